```python
import math
import jax, jax.numpy as jnp
from jax import lax
import numpy as np

D_MODEL = 1024
BATCH = 32
SEQ = 2048
DEPTH = 1

N_Q_HEADS = 8
N_KV_HEADS = 2
HEAD_DIM = 64
GQA_GROUP = N_Q_HEADS // N_KV_HEADS
ATTN_WIDTH = N_Q_HEADS * HEAD_DIM
KV_WIDTH = N_KV_HEADS * HEAD_DIM
WINDOW = 128
BLOCK = 128
N_BUCKETS = 32
MAX_DISTANCE = 128
POOL_WINDOWS = (2, 4, 8, 16)
N_POOL_GROUPS = len(POOL_WINDOWS)
POOL_WIDTH = D_MODEL // 2
POOL_GROUP_DIM = POOL_WIDTH // N_POOL_GROUPS
MIX_WIDTH = ATTN_WIDTH + POOL_WIDTH
IN_WIDTH = ATTN_WIDTH + 2 * KV_WIDTH + POOL_WIDTH
D_FF = 2816
MACARON_WEIGHT = 0.5
PLE_DIM = 256
EPS = 1e-6
NEG_INF = -1e30

kernel_name = "hybrid_swa_sink_pool_macaron_ple"


def rms_norm(x, gain):
    xf = x.astype(jnp.float32)
    y = xf * lax.rsqrt(jnp.mean(xf * xf, axis=-1, keepdims=True) + EPS)
    return (y * gain.astype(jnp.float32)).astype(x.dtype)


def swiglu(h, w_gu, w_down):
    g, u = jnp.split(h @ w_gu, 2, axis=-1)
    return (jax.nn.silu(g) * u) @ w_down


def t5_bucket(dist):
    n = jnp.maximum(dist, 0)
    max_exact = N_BUCKETS // 2
    nf = jnp.maximum(n, 1).astype(jnp.float32)
    large = max_exact + (jnp.log(nf / max_exact) / math.log(MAX_DISTANCE / max_exact)
                         * (N_BUCKETS - max_exact)).astype(jnp.int32)
    large = jnp.minimum(large, N_BUCKETS - 1)
    return jnp.where(n < max_exact, n, large)


def sliding_window_attention(q, k, v, rel_bias, sinks):
    B, S = q.shape[0], q.shape[1]
    nb = S // BLOCK
    q = q.reshape(B, nb, BLOCK, N_KV_HEADS, GQA_GROUP, HEAD_DIM)
    k = k.reshape(B, nb, BLOCK, N_KV_HEADS, HEAD_DIM)
    v = v.reshape(B, nb, BLOCK, N_KV_HEADS, HEAD_DIM)
    pad = ((0, 0), (1, 0), (0, 0), (0, 0), (0, 0))
    k_ctx = jnp.concatenate([jnp.pad(k, pad)[:, :-1], k], axis=2)
    v_ctx = jnp.concatenate([jnp.pad(v, pad)[:, :-1], v], axis=2)
    scores = jnp.einsum('bnqhgd,bnkhd->bnhgqk', q, k_ctx).astype(jnp.float32)
    scores = scores * (1.0 / math.sqrt(HEAD_DIM))
    qi = jnp.arange(BLOCK)[:, None]
    kj = jnp.arange(2 * BLOCK)[None, :]
    dist = qi + BLOCK - kj
    bias = rel_bias[t5_bucket(dist)]
    bias = jnp.transpose(bias, (2, 0, 1)).reshape(
        N_KV_HEADS, GQA_GROUP, BLOCK, 2 * BLOCK).astype(jnp.float32)
    key_pos = jnp.arange(nb)[:, None, None] * BLOCK - BLOCK + kj[None]
    valid = ((dist >= 0) & (dist < WINDOW))[None] & (key_pos >= 0)
    scores = jnp.where(valid[None, :, None, None], scores + bias, NEG_INF)
    sink = jnp.broadcast_to(
        sinks.astype(jnp.float32).reshape(N_KV_HEADS, GQA_GROUP)[None, None, :, :, None, None],
        scores.shape[:-1] + (1,))
    probs = jax.nn.softmax(jnp.concatenate([scores, sink], axis=-1), axis=-1)[..., :-1]
    out = jnp.einsum('bnhgqk,bnkhd->bnqhgd', probs.astype(v.dtype), v_ctx)
    return out.reshape(B, S, ATTN_WIDTH)


def multiscale_pool(u, pool_w, pool_scale):
    B, S, _ = u.shape
    uf = u.astype(jnp.float32)
    c = jnp.pad(jnp.cumsum(uf, axis=1), ((0, 0), (1, 0), (0, 0)))
    t = jnp.arange(S)
    means = []
    for g, w in enumerate(POOL_WINDOWS):
        cg = c[..., g * POOL_GROUP_DIM:(g + 1) * POOL_GROUP_DIM]
        lo = jnp.maximum(t + 1 - w, 0)
        s = cg[:, 1:] - jnp.take(cg, lo, axis=1)
        cnt = (t + 1 - lo).astype(jnp.float32)[None, :, None]
        means.append(s / cnt)
    mean = jnp.stack(means, axis=2)
    d = (mean - uf.reshape(B, S, N_POOL_GROUPS, POOL_GROUP_DIM)).astype(u.dtype)
    y = jnp.einsum('bsgc,gcd->bsgd', d, pool_w).reshape(B, S, POOL_WIDTH)
    return y * pool_scale


def setup_inputs(seed: int = 0) -> dict:
    key = jax.random.key(seed)
    ks = jax.random.split(key, 24)
    f32 = jnp.float32

    def nrm(k, shape, scale):
        return jax.random.normal(k, shape, f32) * scale

    def gain(k, shape):
        return 1.0 + 0.05 * jax.random.normal(k, shape, f32)

    L = DEPTH
    return {
        "x": nrm(ks[0], (BATCH, SEQ, D_MODEL), 1.0),
        "p": nrm(ks[1], (DEPTH, BATCH, SEQ, PLE_DIM), 1.0),
        "ffn1_norm": gain(ks[2], (L, D_MODEL)),
        "ffn1_w_gu": nrm(ks[3], (L, D_MODEL, 2 * D_FF), D_MODEL ** -0.5),
        "ffn1_w_down": nrm(ks[4], (L, D_FF, D_MODEL), D_FF ** -0.5),
        "mix_norm": gain(ks[5], (L, D_MODEL)),
        "w_in": nrm(ks[6], (L, D_MODEL, IN_WIDTH), D_MODEL ** -0.5),
        "q_norm": gain(ks[7], (L, HEAD_DIM)),
        "k_norm": gain(ks[8], (L, HEAD_DIM)),
        "rel_bias": nrm(ks[9], (N_BUCKETS, N_Q_HEADS), 0.5),
        "sinks": nrm(ks[10], (L, N_Q_HEADS), 1.0),
        "pool_w": nrm(ks[11], (L, N_POOL_GROUPS, POOL_GROUP_DIM, POOL_GROUP_DIM), POOL_GROUP_DIM ** -0.5),
        "pool_scale": gain(ks[12], (L, POOL_WIDTH)),
        "w_out": nrm(ks[13], (L, MIX_WIDTH, D_MODEL), MIX_WIDTH ** -0.5),
        "ffn2_norm": gain(ks[14], (L, D_MODEL)),
        "ffn2_w_gu": nrm(ks[15], (L, D_MODEL, 2 * D_FF), D_MODEL ** -0.5),
        "ffn2_w_down": nrm(ks[16], (L, D_FF, D_MODEL), D_FF ** -0.5),
        "ple_norm": gain(ks[17], (L, D_MODEL)),
        "ple_w_gate": nrm(ks[18], (L, D_MODEL, D_MODEL), D_MODEL ** -0.5),
        "ple_b_gate": nrm(ks[19], (L, D_MODEL), 0.02),
        "ple_w_proj": nrm(ks[20], (L, PLE_DIM, D_MODEL), PLE_DIM ** -0.5),
        "ple_post_norm": gain(ks[21], (L, D_MODEL)),
    }


def reference(x, p, ffn1_norm, ffn1_w_gu, ffn1_w_down, mix_norm, w_in, q_norm, k_norm,
              rel_bias, sinks, pool_w, pool_scale, w_out, ffn2_norm, ffn2_w_gu, ffn2_w_down,
              ple_norm, ple_w_gate, ple_b_gate, ple_w_proj, ple_post_norm):
    B, S, _ = x.shape
    for i in range(DEPTH):
        x = x + MACARON_WEIGHT * swiglu(rms_norm(x, ffn1_norm[i]), ffn1_w_gu[i], ffn1_w_down[i])
        h = rms_norm(x, mix_norm[i])
        proj = h @ w_in[i]
        q, k, v, u = jnp.split(
            proj, [ATTN_WIDTH, ATTN_WIDTH + KV_WIDTH, ATTN_WIDTH + 2 * KV_WIDTH], axis=-1)
        q = rms_norm(q.reshape(B, S, N_Q_HEADS, HEAD_DIM), q_norm[i])
        k = rms_norm(k.reshape(B, S, N_KV_HEADS, HEAD_DIM), k_norm[i])
        v = v.reshape(B, S, N_KV_HEADS, HEAD_DIM)
        a = sliding_window_attention(q, k, v, rel_bias, sinks[i])
        m = multiscale_pool(u, pool_w[i], pool_scale[i])
        x = x + jnp.concatenate([a, m], axis=-1) @ w_out[i]
        x = x + MACARON_WEIGHT * swiglu(rms_norm(x, ffn2_norm[i]), ffn2_w_gu[i], ffn2_w_down[i])
        gate = jax.nn.sigmoid(rms_norm(x, ple_norm[i]) @ ple_w_gate[i] + ple_b_gate[i])
        e = rms_norm(p[i] @ ple_w_proj[i], ple_post_norm[i])
        x = x + gate * e
    return x
```

```python
import functools
import math

import jax
import jax.numpy as jnp
from jax import lax
from jax.experimental import pallas as pl
from jax.experimental.pallas import tpu as pltpu

F32 = jnp.float32
BF16 = jnp.bfloat16

WINDOW = 128
MAX_DISTANCE = 128
POOL_WINDOWS = (2, 4, 8, 16)
MACARON_WEIGHT = 0.5
EPS = 1e-6
NEG_INF = -1e30

V7X_LANES = 128
V7X_MXU_DIM = 256

FFN_ROWS = 512
MIX_ROWS = 512
MAX_POOL = max(POOL_WINDOWS)


def _rms(x, gain):
    ms = jnp.mean(x * x, axis=-1, keepdims=True)
    return x * lax.rsqrt(ms + EPS) * gain


def _const_spec(shape):
    zeros = (0,) * len(shape)
    return pl.BlockSpec(shape, lambda *_: zeros, pipeline_mode=pl.Buffered(1))


def _ffn_kernel(x_ref, gain_ref, wgu_ref, wd_ref, o_ref, h_ref, acc_ref, *, n_chunks, chunk):
    h_ref[...] = _rms(x_ref[...], gain_ref[...]).astype(BF16)
    acc_ref[...] = jnp.zeros_like(acc_ref)

    def body(c, carry):
        gu = jnp.dot(h_ref[...], wgu_ref[c], preferred_element_type=F32)
        g = gu[:, :chunk]
        u = gu[:, chunk:]
        act = (g * jax.nn.sigmoid(g)) * u
        acc_ref[...] += jnp.dot(act.astype(BF16), wd_ref[c], preferred_element_type=F32)
        return carry

    lax.fori_loop(0, n_chunks, body, 0)
    o_ref[...] = x_ref[...] + MACARON_WEIGHT * acc_ref[...]


def _ffn(x2, gain, w_gu, w_down):
    n_tok, d = x2.shape
    d_ff = w_down.shape[0]
    chunk = V7X_MXU_DIM
    n_chunks = d_ff // chunk
    assert n_chunks * chunk == d_ff and n_tok % FFN_ROWS == 0
    wg = w_gu[:, :d_ff].reshape(d, n_chunks, chunk)
    wu = w_gu[:, d_ff:].reshape(d, n_chunks, chunk)
    wgu = jnp.concatenate([wg, wu], axis=-1).transpose(1, 0, 2).astype(BF16)
    wd = w_down.reshape(n_chunks, chunk, d).astype(BF16)

    row_spec = pl.BlockSpec((FFN_ROWS, d), lambda i: (i, 0))
    return pl.pallas_call(
        functools.partial(_ffn_kernel, n_chunks=n_chunks, chunk=chunk),
        name="ffn",
        grid=(n_tok // FFN_ROWS,),
        in_specs=[row_spec, _const_spec((1, d)), _const_spec(wgu.shape), _const_spec(wd.shape)],
        out_specs=row_spec,
        out_shape=jax.ShapeDtypeStruct((n_tok, d), F32),
        scratch_shapes=[pltpu.VMEM((FFN_ROWS, d), BF16), pltpu.VMEM((FFN_ROWS, d), F32)],
        compiler_params=pltpu.CompilerParams(
            dimension_semantics=("arbitrary",), vmem_limit_bytes=48 * 1024 * 1024),
    )(x2, gain.reshape(1, d), wgu, wd)


def _head_rms(t, ones_bd, gain, head_dim):
    sq = t * t
    hi = sq.astype(BF16)
    lo = (sq - hi.astype(F32)).astype(BF16)
    ss = (jnp.dot(hi, ones_bd, preferred_element_type=F32)
          + jnp.dot(lo, ones_bd, preferred_element_type=F32))
    return t * lax.rsqrt(ss * (1.0 / head_dim) + EPS) * gain


def _dup_half(t, lane, half):
    r = pltpu.roll(t, half, axis=1)
    low = lane < half
    return jnp.where(low, t, r), jnp.where(low, r, t)


def _mixer_kernel(bucket_ref, relb_ref, sinks_ref, x_ref, gain_ref, win_ref, qg_ref, kg_ref,
                  bd_ref, poolw_ref, pscale_ref, wout_ref, o_ref,
                  tbl_ref, qs_ref, kctx_ref, vctx_ref, uext_ref, cat_ref,
                  *, rows, n_heads, n_kv, head_dim, n_buckets):
    blk = WINDOW
    n_blk = rows // blk
    group = n_heads // n_kv
    attn_w = n_heads * head_dim
    kv_w = n_kv * head_dim
    pool_w = cat_ref.shape[1] - attn_w
    pool_g = pool_w // len(POOL_WINDOWS)
    j = pl.program_id(1)

    @pl.when((pl.program_id(0) == 0) & (j == 0))
    def _build_bias():
        def rows8(r, carry):
            r0 = pl.multiple_of(r * 8, 8)
            bidx = bucket_ref[pl.ds(r0, 8), :]
            accs = [jnp.zeros(bidx.shape, F32) for _ in range(n_heads)]
            for b in range(n_buckets):
                hit = bidx == b
                accs = [jnp.where(hit, relb_ref[b, h], accs[h]) for h in range(n_heads)]
            for h in range(n_heads):
                tbl_ref[h, pl.ds(r0, 8), :] = accs[h]
            return carry
        lax.fori_loop(0, blk // 8, rows8, 0)

    @pl.when(j == 0)
    def _reset_carry():
        kctx_ref[:, 0:blk, :] = jnp.zeros((n_kv, blk, V7X_LANES), BF16)
        vctx_ref[:, 0:blk, :] = jnp.zeros((n_kv, blk, V7X_LANES), BF16)
        uext_ref[0:MAX_POOL, :] = jnp.zeros((MAX_POOL, pool_w), F32)

    x = x_ref[...]
    h = _rms(x, gain_ref[...]).astype(BF16)
    proj = jnp.dot(h, win_ref[...], preferred_element_type=F32)
    q = proj[:, 0:attn_w]
    k = proj[:, attn_w:attn_w + kv_w]
    v = proj[:, attn_w + kv_w:attn_w + 2 * kv_w]
    u = proj[:, attn_w + 2 * kv_w:]

    lane = lax.broadcasted_iota(jnp.int32, (rows, V7X_LANES), 1)
    low = lane < head_dim

    qn = _head_rms(q, bd_ref[...], qg_ref[...], head_dim) * (1.0 / math.sqrt(head_dim))
    for c in range(attn_w // V7X_LANES):
        qc = qn[:, c * V7X_LANES:(c + 1) * V7X_LANES]
        q_lo = jnp.where(low, qc, 0.0).astype(BF16)
        q_hi = jnp.where(low, 0.0, qc).astype(BF16)
        kvh, slot = divmod(2 * c, group)
        for b in range(n_blk):
            qs_ref[kvh, b, slot * blk:(slot + 1) * blk, :] = q_lo[b * blk:(b + 1) * blk]
            qs_ref[kvh, b, (slot + 1) * blk:(slot + 2) * blk, :] = q_hi[b * blk:(b + 1) * blk]

    kn = _head_rms(k, bd_ref[0:kv_w, 0:kv_w], kg_ref[...], head_dim)
    for kvh, (kd, vd) in enumerate(zip(_dup_half(kn, lane, head_dim), _dup_half(v, lane, head_dim))):
        kctx_ref[kvh, blk:blk + rows, :] = kd.astype(BF16)
        vctx_ref[kvh, blk:blk + rows, :] = vd.astype(BF16)

    qi = lax.broadcasted_iota(jnp.int32, (blk, 2 * blk), 0)
    kj = lax.broadcasted_iota(jnp.int32, (blk, 2 * blk), 1)
    dist = qi + blk - kj
    band = (dist >= 0) & (dist < WINDOW)
    low_blk = lax.broadcasted_iota(jnp.int32, (blk, V7X_LANES), 1) < head_dim

    def attend(b, carry):
        r0 = pl.multiple_of(b * blk, blk)
        has_prev = (j > 0) | (b > 0)
        valid = band & ((kj >= blk) | has_prev)
        for kvh in range(n_kv):
            kc = kctx_ref[kvh, pl.ds(r0, 2 * blk), :]
            vc = vctx_ref[kvh, pl.ds(r0, 2 * blk), :]
            s = lax.dot_general(qs_ref[kvh, b], kc, (((1,), (1,)), ((), ())),
                                preferred_element_type=F32)
            es, inv = [], []
            for i in range(group):
                head = kvh * group + i
                si = jnp.where(valid, s[i * blk:(i + 1) * blk] + tbl_ref[head], NEG_INF)
                sink = sinks_ref[head]
                m = jnp.maximum(jnp.max(si, axis=-1, keepdims=True), sink)
                e = jnp.exp(si - m)
                den = jnp.sum(e, axis=-1, keepdims=True) + jnp.exp(sink - m)
                es.append(e.astype(BF16))
                inv.append(1.0 / den)
            pv = jnp.dot(jnp.concatenate(es, axis=0), vc, preferred_element_type=F32)
            outs = [pv[i * blk:(i + 1) * blk] * inv[i] for i in range(group)]
            for pair in range(group // 2):
                col = (kvh * group + 2 * pair) * head_dim
                both = jnp.where(low_blk, outs[2 * pair], outs[2 * pair + 1])
                cat_ref[pl.ds(r0, blk), col:col + V7X_LANES] = both.astype(BF16)
        return carry

    lax.fori_loop(0, n_blk, attend, 0)

    uext_ref[MAX_POOL:MAX_POOL + rows, :] = u
    t_pos = j * rows + lax.broadcasted_iota(jnp.int32, (rows, 1), 0)
    for g, w in enumerate(POOL_WINDOWS):
        cols = slice(g * pool_g, (g + 1) * pool_g)
        ug = u[:, cols]
        total = ug
        for back in range(1, w):
            total = total + uext_ref[MAX_POOL - back:MAX_POOL - back + rows, cols]
        cnt = jnp.minimum(t_pos + 1, w).astype(F32)
        d = total * (1.0 / cnt) - ug
        y = jnp.dot(d.astype(BF16), poolw_ref[g], preferred_element_type=F32)
        cat_ref[:, attn_w + g * pool_g:attn_w + (g + 1) * pool_g] = (y * pscale_ref[:, cols]).astype(BF16)

    o_ref[...] = x + jnp.dot(cat_ref[...], wout_ref[...], preferred_element_type=F32)

    kctx_ref[:, 0:blk, :] = kctx_ref[:, rows:rows + blk, :]
    vctx_ref[:, 0:blk, :] = vctx_ref[:, rows:rows + blk, :]
    uext_ref[0:MAX_POOL, :] = uext_ref[rows:rows + MAX_POOL, :]


def _t5_bucket_table(n_buckets):
    qi = jnp.arange(WINDOW)[:, None]
    kj = jnp.arange(2 * WINDOW)[None, :]
    n = jnp.maximum(qi + WINDOW - kj, 0)
    max_exact = n_buckets // 2
    nf = jnp.maximum(n, 1).astype(F32)
    large = max_exact + (jnp.log(nf / max_exact) / math.log(MAX_DISTANCE / max_exact)
                         * (n_buckets - max_exact)).astype(jnp.int32)
    large = jnp.minimum(large, n_buckets - 1)
    return jnp.where(n < max_exact, n, large).astype(jnp.int32)


def _mixer(x2, seq, gain, w_in, q_norm, k_norm, rel_bias, sinks, pool_w, pool_scale, w_out):
    n_tok, d = x2.shape
    head_dim = q_norm.shape[0]
    n_buckets, n_heads = rel_bias.shape
    n_groups, pool_g, _ = pool_w.shape
    attn_w = n_heads * head_dim
    pool_width = n_groups * pool_g
    kv_w = (w_in.shape[1] - attn_w - pool_width) // 2
    n_kv = kv_w // head_dim
    group = n_heads // n_kv
    rows = MIX_ROWS
    assert n_groups == len(POOL_WINDOWS) and pool_g == V7X_LANES
    assert 2 * head_dim == V7X_LANES and kv_w == V7X_LANES and group % 2 == 0
    assert math.log2(head_dim) % 2 == 0, "score scale must be a power of two to fold into q"
    assert seq % rows == 0 and rows % WINDOW == 0 and w_out.shape[0] == attn_w + pool_width
    tiles = seq // rows

    head_id = jnp.arange(attn_w) // head_dim
    ones_bd = (head_id[:, None] == head_id[None, :]).astype(BF16)

    row_spec = pl.BlockSpec((rows, d), lambda b, j: (b * tiles + j, 0))
    smem = pl.BlockSpec(memory_space=pltpu.SMEM)
    kernel = functools.partial(_mixer_kernel, rows=rows, n_heads=n_heads, n_kv=n_kv,
                               head_dim=head_dim, n_buckets=n_buckets)
    return pl.pallas_call(
        kernel,
        name="mixer",
        grid=(n_tok // seq, tiles),
        in_specs=[
            _const_spec((WINDOW, 2 * WINDOW)), smem, smem,
            row_spec, _const_spec((1, d)), _const_spec(w_in.shape),
            _const_spec((1, attn_w)), _const_spec((1, kv_w)), _const_spec((attn_w, attn_w)),
            _const_spec(pool_w.shape), _const_spec((1, pool_width)), _const_spec(w_out.shape),
        ],
        out_specs=row_spec,
        out_shape=jax.ShapeDtypeStruct((n_tok, d), F32),
        scratch_shapes=[
            pltpu.VMEM((n_heads, WINDOW, 2 * WINDOW), F32),
            pltpu.VMEM((n_kv, rows // WINDOW, group * WINDOW, V7X_LANES), BF16),
            pltpu.VMEM((n_kv, WINDOW + rows, V7X_LANES), BF16),
            pltpu.VMEM((n_kv, WINDOW + rows, V7X_LANES), BF16),
            pltpu.VMEM((MAX_POOL + rows, pool_width), F32),
            pltpu.VMEM((rows, attn_w + pool_width), BF16),
        ],
        compiler_params=pltpu.CompilerParams(
            dimension_semantics=("arbitrary", "arbitrary"), vmem_limit_bytes=48 * 1024 * 1024),
    )(_t5_bucket_table(n_buckets), rel_bias, sinks,
      x2, gain.reshape(1, d), w_in.astype(BF16),
      jnp.tile(q_norm, n_heads).reshape(1, attn_w), jnp.tile(k_norm, n_kv).reshape(1, kv_w),
      ones_bd, pool_w.astype(BF16), pool_scale.reshape(1, pool_width), w_out.astype(BF16))


def _ple_kernel(x_ref, p_ref, gain_ref, wg_ref, bg_ref, wp_ref, post_ref, o_ref):
    x = x_ref[...]
    h = _rms(x, gain_ref[...]).astype(BF16)
    gate = jax.nn.sigmoid(jnp.dot(h, wg_ref[...], preferred_element_type=F32) + bg_ref[...])
    e = jnp.dot(p_ref[...].astype(BF16), wp_ref[...], preferred_element_type=F32)
    o_ref[...] = x + gate * _rms(e, post_ref[...])


def _ple(x2, p2, gain, w_gate, b_gate, w_proj, post_gain):
    n_tok, d = x2.shape
    ple_dim = p2.shape[1]
    row_spec = pl.BlockSpec((FFN_ROWS, d), lambda i: (i, 0))
    return pl.pallas_call(
        _ple_kernel,
        name="ple",
        grid=(n_tok // FFN_ROWS,),
        in_specs=[row_spec, pl.BlockSpec((FFN_ROWS, ple_dim), lambda i: (i, 0)),
                  _const_spec((1, d)), _const_spec(w_gate.shape), _const_spec((1, d)),
                  _const_spec(w_proj.shape), _const_spec((1, d))],
        out_specs=row_spec,
        out_shape=jax.ShapeDtypeStruct((n_tok, d), F32),
        compiler_params=pltpu.CompilerParams(
            dimension_semantics=("arbitrary",), vmem_limit_bytes=32 * 1024 * 1024),
    )(x2, p2, gain.reshape(1, d), w_gate.astype(BF16), b_gate.reshape(1, d),
      w_proj.astype(BF16), post_gain.reshape(1, d))


def kernel(x, p, ffn1_norm, ffn1_w_gu, ffn1_w_down, mix_norm, w_in, q_norm, k_norm, rel_bias,
           sinks, pool_w, pool_scale, w_out, ffn2_norm, ffn2_w_gu, ffn2_w_down, ple_norm,
           ple_w_gate, ple_b_gate, ple_w_proj, ple_post_norm):
    batch, seq, d = x.shape
    x2 = x.reshape(batch * seq, d)
    for i in range(p.shape[0]):
        x2 = _ffn(x2, ffn1_norm[i], ffn1_w_gu[i], ffn1_w_down[i])
        x2 = _mixer(x2, seq, mix_norm[i], w_in[i], q_norm[i], k_norm[i], rel_bias, sinks[i],
                    pool_w[i], pool_scale[i], w_out[i])
        x2 = _ffn(x2, ffn2_norm[i], ffn2_w_gu[i], ffn2_w_down[i])
        x2 = _ple(x2, p[i].reshape(batch * seq, -1), ple_norm[i], ple_w_gate[i], ple_b_gate[i],
                  ple_w_proj[i], ple_post_norm[i])
    return x2.reshape(batch, seq, d)
```

```python
import functools
import math

import jax
import jax.numpy as jnp
from jax import lax
from jax.experimental import pallas as pl
from jax.experimental.pallas import tpu as pltpu

F32 = jnp.float32
BF16 = jnp.bfloat16

WINDOW = 128
MAX_DISTANCE = 128
POOL_WINDOWS = (2, 4, 8, 16)
MACARON_WEIGHT = 0.5
EPS = 1e-6
NEG_INF = -1e30

V7X_LANES = 128
V7X_MXU_DIM = 256

FFN_ROWS = 512
MIX_ROWS = 512
MAX_POOL = max(POOL_WINDOWS)


def _rms(x, gain):
    ms = jnp.mean(x * x, axis=-1, keepdims=True)
    return x * lax.rsqrt(ms + EPS) * gain


def _const_spec(shape):
    zeros = (0,) * len(shape)
    return pl.BlockSpec(shape, lambda *_: zeros, pipeline_mode=pl.Buffered(1))


def _ffn_kernel(x_ref, gain_ref, wgu_ref, wd_ref, *rest, d_ff, chunk, with_ple):
    if with_ple:
        p_ref, pgain_ref, wgate_ref, bgate_ref, wproj_ref, post_ref, o_ref, h_ref, acc_ref = rest
    else:
        o_ref, h_ref, acc_ref = rest
    h_ref[...] = _rms(x_ref[...], gain_ref[...]).astype(BF16)

    n_chunks = d_ff // chunk
    for c in range(n_chunks):
        cols = slice(c * chunk, (c + 1) * chunk)
        ucols = slice(d_ff + c * chunk, d_ff + (c + 1) * chunk)
        g = jnp.dot(h_ref[...], wgu_ref[:, cols], preferred_element_type=F32)
        u = jnp.dot(h_ref[...], wgu_ref[:, ucols], preferred_element_type=F32)
        act = (g * jax.nn.sigmoid(g)) * u
        down = jnp.dot(act.astype(BF16), wd_ref[cols, :], preferred_element_type=F32)
        if c == 0:
            acc_ref[...] = down
        elif c < n_chunks - 1:
            acc_ref[...] += down
        else:
            x1 = x_ref[...] + MACARON_WEIGHT * (acc_ref[...] + down)

    if not with_ple:
        o_ref[...] = x1
        return
    hp = _rms(x1, pgain_ref[...]).astype(BF16)
    gate = jax.nn.sigmoid(jnp.dot(hp, wgate_ref[...], preferred_element_type=F32) + bgate_ref[...])
    e = jnp.dot(p_ref[...].astype(BF16), wproj_ref[...], preferred_element_type=F32)
    o_ref[...] = x1 + gate * _rms(e, post_ref[...])


def _ffn(x2, gain, w_gu, w_down, ple=None):
    n_tok, d = x2.shape
    d_ff = w_down.shape[0]
    chunk = V7X_MXU_DIM
    assert d_ff % chunk == 0 and n_tok % FFN_ROWS == 0 and w_gu.shape == (d, 2 * d_ff)
    row_spec = pl.BlockSpec((FFN_ROWS, d), lambda i: (i, 0))
    in_specs = [row_spec, _const_spec((1, d)), _const_spec(w_gu.shape), _const_spec(w_down.shape)]
    args = [x2, gain.reshape(1, d), w_gu.astype(BF16), w_down.astype(BF16)]
    if ple is not None:
        p2, pgain, w_gate, b_gate, w_proj, post_gain = ple
        in_specs += [pl.BlockSpec((FFN_ROWS, p2.shape[1]), lambda i: (i, 0)), _const_spec((1, d)),
                     _const_spec(w_gate.shape), _const_spec((1, d)), _const_spec(w_proj.shape),
                     _const_spec((1, d))]
        args += [p2, pgain.reshape(1, d), w_gate.astype(BF16), b_gate.reshape(1, d),
                 w_proj.astype(BF16), post_gain.reshape(1, d)]
    return pl.pallas_call(
        functools.partial(_ffn_kernel, d_ff=d_ff, chunk=chunk, with_ple=ple is not None),
        name="ffn_ple" if ple is not None else "ffn",
        grid=(n_tok // FFN_ROWS,),
        in_specs=in_specs,
        out_specs=row_spec,
        out_shape=jax.ShapeDtypeStruct((n_tok, d), F32),
        scratch_shapes=[pltpu.VMEM((FFN_ROWS, d), BF16), pltpu.VMEM((FFN_ROWS, d), F32)],
        compiler_params=pltpu.CompilerParams(
            dimension_semantics=("arbitrary",), vmem_limit_bytes=48 * 1024 * 1024),
    )(*args)


def _head_rms(t, ones_bd, gain, head_dim):
    sq = t * t
    hi = sq.astype(BF16)
    lo = (sq - hi.astype(F32)).astype(BF16)
    ss = (jnp.dot(hi, ones_bd, preferred_element_type=F32)
          + jnp.dot(lo, ones_bd, preferred_element_type=F32))
    return t * lax.rsqrt(ss * (1.0 / head_dim) + EPS) * gain


def _dup_half(t, lane, half):
    r = pltpu.roll(t, half, axis=1)
    low = lane < half
    return jnp.where(low, t, r), jnp.where(low, r, t)


def _mixer_kernel(bucket_ref, relb_ref, sinks_ref, x_ref, gain_ref, win_ref, qg_ref, kg_ref,
                  bd_ref, poolw_ref, pscale_ref, wout_ref, o_ref,
                  tbl_ref, qs_ref, kctx_ref, vctx_ref, uext_ref, cat_ref,
                  *, rows, n_heads, n_kv, head_dim, n_buckets):
    blk = WINDOW
    n_blk = rows // blk
    group = n_heads // n_kv
    attn_w = n_heads * head_dim
    kv_w = n_kv * head_dim
    pool_w = cat_ref.shape[1] - attn_w
    pool_g = pool_w // len(POOL_WINDOWS)
    j = pl.program_id(1)

    @pl.when((pl.program_id(0) == 0) & (j == 0))
    def _build_bias():
        def rows8(r, carry):
            r0 = pl.multiple_of(r * 8, 8)
            bidx = bucket_ref[pl.ds(r0, 8), :]
            accs = [jnp.zeros(bidx.shape, F32) for _ in range(n_heads)]
            for b in range(n_buckets):
                hit = bidx == b
                accs = [jnp.where(hit, relb_ref[b, h], accs[h]) for h in range(n_heads)]
            for h in range(n_heads):
                tbl_ref[h, pl.ds(r0, 8), :] = accs[h]
            return carry
        lax.fori_loop(0, blk // 8, rows8, 0)

    @pl.when(j == 0)
    def _reset_carry():
        kctx_ref[:, 0:blk, :] = jnp.zeros((n_kv, blk, V7X_LANES), BF16)
        vctx_ref[:, 0:blk, :] = jnp.zeros((n_kv, blk, V7X_LANES), BF16)
        uext_ref[0:MAX_POOL, :] = jnp.zeros((MAX_POOL, pool_w), F32)

    x = x_ref[...]
    h = _rms(x, gain_ref[...]).astype(BF16)
    proj = jnp.dot(h, win_ref[...], preferred_element_type=F32)
    q = proj[:, 0:attn_w]
    k = proj[:, attn_w:attn_w + kv_w]
    v = proj[:, attn_w + kv_w:attn_w + 2 * kv_w]
    u = proj[:, attn_w + 2 * kv_w:]

    lane = lax.broadcasted_iota(jnp.int32, (rows, V7X_LANES), 1)
    low = lane < head_dim

    qn = _head_rms(q, bd_ref[...], qg_ref[...], head_dim) * (1.0 / math.sqrt(head_dim))
    for c in range(attn_w // V7X_LANES):
        qc = qn[:, c * V7X_LANES:(c + 1) * V7X_LANES]
        q_lo = jnp.where(low, qc, 0.0).astype(BF16)
        q_hi = jnp.where(low, 0.0, qc).astype(BF16)
        kvh, slot = divmod(2 * c, group)
        for b in range(n_blk):
            qs_ref[kvh, b, slot * blk:(slot + 1) * blk, :] = q_lo[b * blk:(b + 1) * blk]
            qs_ref[kvh, b, (slot + 1) * blk:(slot + 2) * blk, :] = q_hi[b * blk:(b + 1) * blk]

    kn = _head_rms(k, bd_ref[0:kv_w, 0:kv_w], kg_ref[...], head_dim)
    for kvh, (kd, vd) in enumerate(zip(_dup_half(kn, lane, head_dim), _dup_half(v, lane, head_dim))):
        kctx_ref[kvh, blk:blk + rows, :] = kd.astype(BF16)
        vctx_ref[kvh, blk:blk + rows, :] = vd.astype(BF16)

    qi = lax.broadcasted_iota(jnp.int32, (blk, 2 * blk), 0)
    kj = lax.broadcasted_iota(jnp.int32, (blk, 2 * blk), 1)
    dist = qi + blk - kj
    band = (dist >= 0) & (dist < WINDOW)
    low_blk = lax.broadcasted_iota(jnp.int32, (blk, V7X_LANES), 1) < head_dim

    def attend(b, carry):
        r0 = pl.multiple_of(b * blk, blk)
        has_prev = (j > 0) | (b > 0)
        valid = band & ((kj >= blk) | has_prev)
        for kvh in range(n_kv):
            kc = kctx_ref[kvh, pl.ds(r0, 2 * blk), :]
            vc = vctx_ref[kvh, pl.ds(r0, 2 * blk), :]
            s = lax.dot_general(qs_ref[kvh, b], kc, (((1,), (1,)), ((), ())),
                                preferred_element_type=F32)
            es, inv = [], []
            for i in range(group):
                head = kvh * group + i
                si = jnp.where(valid, s[i * blk:(i + 1) * blk] + tbl_ref[head], NEG_INF)
                sink = sinks_ref[head]
                m = jnp.maximum(jnp.max(si, axis=-1, keepdims=True), sink)
                e = jnp.exp(si - m)
                den = jnp.sum(e, axis=-1, keepdims=True) + jnp.exp(sink - m)
                es.append(e.astype(BF16))
                inv.append(1.0 / den)
            pv = jnp.dot(jnp.concatenate(es, axis=0), vc, preferred_element_type=F32)
            outs = [pv[i * blk:(i + 1) * blk] * inv[i] for i in range(group)]
            for pair in range(group // 2):
                col = (kvh * group + 2 * pair) * head_dim
                both = jnp.where(low_blk, outs[2 * pair], outs[2 * pair + 1])
                cat_ref[pl.ds(r0, blk), col:col + V7X_LANES] = both.astype(BF16)
        return carry

    lax.fori_loop(0, n_blk, attend, 0)

    uext_ref[MAX_POOL:MAX_POOL + rows, :] = u
    t_pos = j * rows + lax.broadcasted_iota(jnp.int32, (rows, 1), 0)
    for g, w in enumerate(POOL_WINDOWS):
        cols = slice(g * pool_g, (g + 1) * pool_g)
        ug = u[:, cols]
        total = ug
        for back in range(1, w):
            total = total + uext_ref[MAX_POOL - back:MAX_POOL - back + rows, cols]
        cnt = jnp.minimum(t_pos + 1, w).astype(F32)
        d = total * (1.0 / cnt) - ug
        y = jnp.dot(d.astype(BF16), poolw_ref[g], preferred_element_type=F32)
        cat_ref[:, attn_w + g * pool_g:attn_w + (g + 1) * pool_g] = (y * pscale_ref[:, cols]).astype(BF16)

    o_ref[...] = x + jnp.dot(cat_ref[...], wout_ref[...], preferred_element_type=F32)

    kctx_ref[:, 0:blk, :] = kctx_ref[:, rows:rows + blk, :]
    vctx_ref[:, 0:blk, :] = vctx_ref[:, rows:rows + blk, :]
    uext_ref[0:MAX_POOL, :] = uext_ref[rows:rows + MAX_POOL, :]


def _t5_bucket_table(n_buckets):
    qi = jnp.arange(WINDOW)[:, None]
    kj = jnp.arange(2 * WINDOW)[None, :]
    n = jnp.maximum(qi + WINDOW - kj, 0)
    max_exact = n_buckets // 2
    nf = jnp.maximum(n, 1).astype(F32)
    large = max_exact + (jnp.log(nf / max_exact) / math.log(MAX_DISTANCE / max_exact)
                         * (n_buckets - max_exact)).astype(jnp.int32)
    large = jnp.minimum(large, n_buckets - 1)
    return jnp.where(n < max_exact, n, large).astype(jnp.int32)


def _mixer(x2, seq, gain, w_in, q_norm, k_norm, rel_bias, sinks, pool_w, pool_scale, w_out):
    n_tok, d = x2.shape
    head_dim = q_norm.shape[0]
    n_buckets, n_heads = rel_bias.shape
    n_groups, pool_g, _ = pool_w.shape
    attn_w = n_heads * head_dim
    pool_width = n_groups * pool_g
    kv_w = (w_in.shape[1] - attn_w - pool_width) // 2
    n_kv = kv_w // head_dim
    group = n_heads // n_kv
    rows = MIX_ROWS
    assert n_groups == len(POOL_WINDOWS) and pool_g == V7X_LANES
    assert 2 * head_dim == V7X_LANES and kv_w == V7X_LANES and group % 2 == 0
    assert math.log2(head_dim) % 2 == 0, "score scale must be a power of two to fold into q"
    assert seq % rows == 0 and rows % WINDOW == 0 and w_out.shape[0] == attn_w + pool_width
    tiles = seq // rows

    head_id = jnp.arange(attn_w) // head_dim
    ones_bd = (head_id[:, None] == head_id[None, :]).astype(BF16)

    row_spec = pl.BlockSpec((rows, d), lambda b, j: (b * tiles + j, 0))
    smem = pl.BlockSpec(memory_space=pltpu.SMEM)
    kernel = functools.partial(_mixer_kernel, rows=rows, n_heads=n_heads, n_kv=n_kv,
                               head_dim=head_dim, n_buckets=n_buckets)
    return pl.pallas_call(
        kernel,
        name="mixer",
        grid=(n_tok // seq, tiles),
        in_specs=[
            _const_spec((WINDOW, 2 * WINDOW)), smem, smem,
            row_spec, _const_spec((1, d)), _const_spec(w_in.shape),
            _const_spec((1, attn_w)), _const_spec((1, kv_w)), _const_spec((attn_w, attn_w)),
            _const_spec(pool_w.shape), _const_spec((1, pool_width)), _const_spec(w_out.shape),
        ],
        out_specs=row_spec,
        out_shape=jax.ShapeDtypeStruct((n_tok, d), F32),
        scratch_shapes=[
            pltpu.VMEM((n_heads, WINDOW, 2 * WINDOW), F32),
            pltpu.VMEM((n_kv, rows // WINDOW, group * WINDOW, V7X_LANES), BF16),
            pltpu.VMEM((n_kv, WINDOW + rows, V7X_LANES), BF16),
            pltpu.VMEM((n_kv, WINDOW + rows, V7X_LANES), BF16),
            pltpu.VMEM((MAX_POOL + rows, pool_width), F32),
            pltpu.VMEM((rows, attn_w + pool_width), BF16),
        ],
        compiler_params=pltpu.CompilerParams(
            dimension_semantics=("arbitrary", "arbitrary"), vmem_limit_bytes=48 * 1024 * 1024),
    )(_t5_bucket_table(n_buckets), rel_bias, sinks,
      x2, gain.reshape(1, d), w_in.astype(BF16),
      jnp.tile(q_norm, n_heads).reshape(1, attn_w), jnp.tile(k_norm, n_kv).reshape(1, kv_w),
      ones_bd, pool_w.astype(BF16), pool_scale.reshape(1, pool_width), w_out.astype(BF16))


def kernel(x, p, ffn1_norm, ffn1_w_gu, ffn1_w_down, mix_norm, w_in, q_norm, k_norm, rel_bias,
           sinks, pool_w, pool_scale, w_out, ffn2_norm, ffn2_w_gu, ffn2_w_down, ple_norm,
           ple_w_gate, ple_b_gate, ple_w_proj, ple_post_norm):
    batch, seq, d = x.shape
    x2 = x.reshape(batch * seq, d)
    for i in range(p.shape[0]):
        x2 = _ffn(x2, ffn1_norm[i], ffn1_w_gu[i], ffn1_w_down[i])
        x2 = _mixer(x2, seq, mix_norm[i], w_in[i], q_norm[i], k_norm[i], rel_bias, sinks[i],
                    pool_w[i], pool_scale[i], w_out[i])
        x2 = _ffn(x2, ffn2_norm[i], ffn2_w_gu[i], ffn2_w_down[i],
                  ple=(p[i].reshape(batch * seq, -1), ple_norm[i], ple_w_gate[i], ple_b_gate[i],
                       ple_w_proj[i], ple_post_norm[i]))
    return x2.reshape(batch, seq, d)
```

```python
import functools
import math

import jax
import jax.numpy as jnp
from jax import lax
from jax.experimental import pallas as pl
from jax.experimental.pallas import tpu as pltpu

F32 = jnp.float32
BF16 = jnp.bfloat16

WINDOW = 128
MAX_DISTANCE = 128
POOL_WINDOWS = (2, 4, 8, 16)
MACARON_WEIGHT = 0.5
EPS = 1e-6
NEG_INF = -1e30
LOG2E = math.log2(math.e)

V7X_LANES = 128
V7X_MXU_DIM = 256

FFN_ROWS = 512
MIX_ROWS = 512
MAX_POOL = max(POOL_WINDOWS)


def _rms(x, gain):
    ms = jnp.mean(x * x, axis=-1, keepdims=True)
    return x * lax.rsqrt(ms + EPS) * gain


def _const_spec(shape):
    zeros = (0,) * len(shape)
    return pl.BlockSpec(shape, lambda *_: zeros, pipeline_mode=pl.Buffered(1))


def _ffn_kernel(x_ref, gain_ref, wgu_ref, wd_ref, *rest, d_ff, chunk, with_ple):
    if with_ple:
        p_ref, pgain_ref, wgate_ref, bgate_ref, wproj_ref, post_ref, o_ref, h_ref, acc_ref = rest
    else:
        o_ref, h_ref, acc_ref = rest
    h_ref[...] = _rms(x_ref[...], gain_ref[...]).astype(BF16)

    n_chunks = d_ff // chunk
    for c in range(n_chunks):
        cols = slice(c * chunk, (c + 1) * chunk)
        ucols = slice(d_ff + c * chunk, d_ff + (c + 1) * chunk)
        g = jnp.dot(h_ref[...], wgu_ref[:, cols], preferred_element_type=F32)
        u = jnp.dot(h_ref[...], wgu_ref[:, ucols], preferred_element_type=F32)
        act = (g * jax.nn.sigmoid(g)) * u
        down = jnp.dot(act.astype(BF16), wd_ref[cols, :], preferred_element_type=F32)
        if c == 0:
            acc_ref[...] = down
        elif c < n_chunks - 1:
            acc_ref[...] += down
        else:
            x1 = x_ref[...] + MACARON_WEIGHT * (acc_ref[...] + down)

    if not with_ple:
        o_ref[...] = x1
        return
    hp = _rms(x1, pgain_ref[...]).astype(BF16)
    gate = jax.nn.sigmoid(jnp.dot(hp, wgate_ref[...], preferred_element_type=F32) + bgate_ref[...])
    e = jnp.dot(p_ref[...].astype(BF16), wproj_ref[...], preferred_element_type=F32)
    o_ref[...] = x1 + gate * _rms(e, post_ref[...])


def _ffn(x2, gain, w_gu, w_down, ple=None):
    n_tok, d = x2.shape
    d_ff = w_down.shape[0]
    chunk = V7X_MXU_DIM
    assert d_ff % chunk == 0 and n_tok % FFN_ROWS == 0 and w_gu.shape == (d, 2 * d_ff)
    row_spec = pl.BlockSpec((FFN_ROWS, d), lambda i: (i, 0))
    in_specs = [row_spec, _const_spec((1, d)), _const_spec(w_gu.shape), _const_spec(w_down.shape)]
    args = [x2, gain.reshape(1, d), w_gu.astype(BF16), w_down.astype(BF16)]
    if ple is not None:
        p2, pgain, w_gate, b_gate, w_proj, post_gain = ple
        in_specs += [pl.BlockSpec((FFN_ROWS, p2.shape[1]), lambda i: (i, 0)), _const_spec((1, d)),
                     _const_spec(w_gate.shape), _const_spec((1, d)), _const_spec(w_proj.shape),
                     _const_spec((1, d))]
        args += [p2, pgain.reshape(1, d), w_gate.astype(BF16), b_gate.reshape(1, d),
                 w_proj.astype(BF16), post_gain.reshape(1, d)]
    return pl.pallas_call(
        functools.partial(_ffn_kernel, d_ff=d_ff, chunk=chunk, with_ple=ple is not None),
        name="ffn_ple" if ple is not None else "ffn",
        grid=(n_tok // FFN_ROWS,),
        in_specs=in_specs,
        out_specs=row_spec,
        out_shape=jax.ShapeDtypeStruct((n_tok, d), F32),
        scratch_shapes=[pltpu.VMEM((FFN_ROWS, d), BF16), pltpu.VMEM((FFN_ROWS, d), F32)],
        compiler_params=pltpu.CompilerParams(
            dimension_semantics=("arbitrary",), vmem_limit_bytes=48 * 1024 * 1024),
    )(*args)


def _head_rms_cols(t, mean2, gain):
    cols = []
    for c in range(t.shape[1] // V7X_LANES):
        sl = slice(c * V7X_LANES, (c + 1) * V7X_LANES)
        tc = t[:, sl]
        sq = tc * tc
        hi = sq.astype(BF16)
        lo = (sq - hi.astype(F32)).astype(BF16)
        ms = jnp.dot(jnp.concatenate([hi, lo], axis=1), mean2, preferred_element_type=F32)
        cols.append(tc * lax.rsqrt(ms + EPS) * gain[:, sl])
    return cols


def _dup_half(t, lane, half):
    r = pltpu.roll(t, half, axis=1)
    low = lane < half
    return jnp.where(low, t, r), jnp.where(low, r, t)


def _mixer_kernel(bucket_ref, relb_ref, sinks_ref, x_ref, gain_ref, win_ref, qg_ref, kg_ref,
                  mean2_ref, poolw_ref, pscale_ref, wout_ref, o_ref,
                  tbl_ref, qs_ref, kctx_ref, vctx_ref, uext_ref, cat_ref,
                  *, rows, n_heads, n_kv, head_dim, n_buckets):
    blk = WINDOW
    n_blk = rows // blk
    group = n_heads // n_kv
    attn_w = n_heads * head_dim
    kv_w = n_kv * head_dim
    pool_w = cat_ref.shape[1] - attn_w
    pool_g = pool_w // len(POOL_WINDOWS)
    j = pl.program_id(1)

    @pl.when((pl.program_id(0) == 0) & (j == 0))
    def _build_bias():
        def rows8(r, carry):
            r0 = pl.multiple_of(r * 8, 8)
            bidx = bucket_ref[pl.ds(r0, 8), :]
            accs = [jnp.zeros(bidx.shape, F32) for _ in range(n_heads)]
            for b in range(n_buckets):
                hit = bidx == b
                accs = [jnp.where(hit, relb_ref[b, h], accs[h]) for h in range(n_heads)]
            for h in range(n_heads):
                tbl_ref[h, pl.ds(r0, 8), :] = accs[h]
            return carry
        lax.fori_loop(0, blk // 8, rows8, 0)

    @pl.when(j == 0)
    def _reset_carry():
        kctx_ref[:, 0:blk, :] = jnp.zeros((n_kv, blk, V7X_LANES), BF16)
        vctx_ref[:, 0:blk, :] = jnp.zeros((n_kv, blk, 2 * V7X_LANES), BF16)
        uext_ref[0:MAX_POOL, :] = jnp.zeros((MAX_POOL, pool_w), F32)

    x = x_ref[...]
    h = _rms(x, gain_ref[...]).astype(BF16)
    proj = jnp.dot(h, win_ref[...], preferred_element_type=F32)
    q = proj[:, 0:attn_w]
    k = proj[:, attn_w:attn_w + kv_w]
    v = proj[:, attn_w + kv_w:attn_w + 2 * kv_w]
    u = proj[:, attn_w + 2 * kv_w:]

    lane = lax.broadcasted_iota(jnp.int32, (rows, V7X_LANES), 1)
    low = lane < head_dim

    for c, qc in enumerate(_head_rms_cols(q, mean2_ref[...], qg_ref[...])):
        q_lo = jnp.where(low, qc, 0.0).astype(BF16)
        q_hi = jnp.where(low, 0.0, qc).astype(BF16)
        kvh, slot = divmod(2 * c, group)
        for b in range(n_blk):
            qs_ref[kvh, b, slot * blk:(slot + 1) * blk, :] = q_lo[b * blk:(b + 1) * blk]
            qs_ref[kvh, b, (slot + 1) * blk:(slot + 2) * blk, :] = q_hi[b * blk:(b + 1) * blk]

    (kn,) = _head_rms_cols(k, mean2_ref[...], kg_ref[...])
    ones = jnp.ones((rows, V7X_LANES), BF16)
    for kvh, (kd, vd) in enumerate(zip(_dup_half(kn, lane, head_dim), _dup_half(v, lane, head_dim))):
        kctx_ref[kvh, blk:blk + rows, :] = kd.astype(BF16)
        vctx_ref[kvh, blk:blk + rows, 0:V7X_LANES] = vd.astype(BF16)
        vctx_ref[kvh, blk:blk + rows, V7X_LANES:] = ones

    qi = lax.broadcasted_iota(jnp.int32, (blk, 2 * blk), 0)
    kj = lax.broadcasted_iota(jnp.int32, (blk, 2 * blk), 1)
    dist = qi + blk - kj
    band = (dist >= 0) & (dist < WINDOW)
    low_blk = lax.broadcasted_iota(jnp.int32, (blk, V7X_LANES), 1) < head_dim

    for b in range(n_blk):
        r0 = b * blk
        valid = band if b > 0 else band & ((kj >= blk) | (j > 0))
        for kvh in range(n_kv):
            kc = kctx_ref[kvh, r0:r0 + 2 * blk, :]
            vc = vctx_ref[kvh, r0:r0 + 2 * blk, :]
            s = lax.dot_general(qs_ref[kvh, b], kc, (((1,), (1,)), ((), ())),
                                preferred_element_type=F32)
            es, ms = [], []
            for i in range(group):
                head = kvh * group + i
                si = jnp.where(valid, s[i * blk:(i + 1) * blk] + tbl_ref[head], NEG_INF)
                m = jnp.maximum(jnp.max(si, axis=-1, keepdims=True), sinks_ref[head])
                es.append(jnp.exp2(si - m).astype(BF16))
                ms.append(m)
            pv = jnp.dot(jnp.concatenate(es, axis=0), vc, preferred_element_type=F32)
            outs = []
            for i in range(group):
                head = kvh * group + i
                pvi = pv[i * blk:(i + 1) * blk]
                den = pvi[:, V7X_LANES:] + jnp.exp2(sinks_ref[head] - ms[i])
                outs.append(pvi[:, 0:V7X_LANES] / den)
            for pair in range(group // 2):
                col = (kvh * group + 2 * pair) * head_dim
                both = jnp.where(low_blk, outs[2 * pair], outs[2 * pair + 1])
                cat_ref[r0:r0 + blk, col:col + V7X_LANES] = both.astype(BF16)

    uext_ref[MAX_POOL:MAX_POOL + rows, :] = u
    t_pos = j * rows + lax.broadcasted_iota(jnp.int32, (rows, 1), 0)
    for g, w in enumerate(POOL_WINDOWS):
        cols = slice(g * pool_g, (g + 1) * pool_g)
        ug = u[:, cols]
        total = uext_ref[:, cols]
        shift = 1
        while shift < w:
            total = total + pltpu.roll(total, shift, axis=0)
            shift *= 2
        cnt = jnp.minimum(t_pos + 1, w).astype(F32)
        d = total[MAX_POOL:] * (1.0 / cnt) - ug
        y = jnp.dot(d.astype(BF16), poolw_ref[g], preferred_element_type=F32)
        cat_ref[:, attn_w + g * pool_g:attn_w + (g + 1) * pool_g] = (y * pscale_ref[:, cols]).astype(BF16)

    o_ref[...] = x + jnp.dot(cat_ref[...], wout_ref[...], preferred_element_type=F32)

    kctx_ref[:, 0:blk, :] = kctx_ref[:, rows:rows + blk, :]
    vctx_ref[:, 0:blk, :] = vctx_ref[:, rows:rows + blk, :]
    uext_ref[0:MAX_POOL, :] = uext_ref[rows:rows + MAX_POOL, :]


def _t5_bucket_table(n_buckets):
    qi = jnp.arange(WINDOW)[:, None]
    kj = jnp.arange(2 * WINDOW)[None, :]
    n = jnp.maximum(qi + WINDOW - kj, 0)
    max_exact = n_buckets // 2
    nf = jnp.maximum(n, 1).astype(F32)
    large = max_exact + (jnp.log(nf / max_exact) / math.log(MAX_DISTANCE / max_exact)
                         * (n_buckets - max_exact)).astype(jnp.int32)
    large = jnp.minimum(large, n_buckets - 1)
    return jnp.where(n < max_exact, n, large).astype(jnp.int32)


def _mixer(x2, seq, gain, w_in, q_norm, k_norm, rel_bias, sinks, pool_w, pool_scale, w_out):
    n_tok, d = x2.shape
    head_dim = q_norm.shape[0]
    n_buckets, n_heads = rel_bias.shape
    n_groups, pool_g, _ = pool_w.shape
    attn_w = n_heads * head_dim
    pool_width = n_groups * pool_g
    kv_w = (w_in.shape[1] - attn_w - pool_width) // 2
    n_kv = kv_w // head_dim
    group = n_heads // n_kv
    rows = MIX_ROWS
    assert n_groups == len(POOL_WINDOWS) and pool_g == V7X_LANES
    assert 2 * head_dim == V7X_LANES and kv_w == V7X_LANES and group % 2 == 0
    assert seq % rows == 0 and rows % WINDOW == 0 and w_out.shape[0] == attn_w + pool_width
    tiles = seq // rows

    assert head_dim & (head_dim - 1) == 0
    lane_head = jnp.arange(V7X_LANES) // head_dim
    mean_bd = jnp.where(lane_head[:, None] == lane_head[None, :], 1.0 / head_dim, 0.0).astype(BF16)
    mean2 = jnp.concatenate([mean_bd, mean_bd], axis=0)
    q_gain = jnp.tile(q_norm, n_heads) * (LOG2E / math.sqrt(head_dim))

    row_spec = pl.BlockSpec((rows, d), lambda b, j: (b * tiles + j, 0))
    smem = pl.BlockSpec(memory_space=pltpu.SMEM)
    kernel = functools.partial(_mixer_kernel, rows=rows, n_heads=n_heads, n_kv=n_kv,
                               head_dim=head_dim, n_buckets=n_buckets)
    return pl.pallas_call(
        kernel,
        name="mixer",
        grid=(n_tok // seq, tiles),
        in_specs=[
            _const_spec((WINDOW, 2 * WINDOW)), smem, smem,
            row_spec, _const_spec((1, d)), _const_spec(w_in.shape),
            _const_spec((1, attn_w)), _const_spec((1, kv_w)), _const_spec(mean2.shape),
            _const_spec(pool_w.shape), _const_spec((1, pool_width)), _const_spec(w_out.shape),
        ],
        out_specs=row_spec,
        out_shape=jax.ShapeDtypeStruct((n_tok, d), F32),
        scratch_shapes=[
            pltpu.VMEM((n_heads, WINDOW, 2 * WINDOW), F32),
            pltpu.VMEM((n_kv, rows // WINDOW, group * WINDOW, V7X_LANES), BF16),
            pltpu.VMEM((n_kv, WINDOW + rows, V7X_LANES), BF16),
            pltpu.VMEM((n_kv, WINDOW + rows, 2 * V7X_LANES), BF16),
            pltpu.VMEM((MAX_POOL + rows, pool_width), F32),
            pltpu.VMEM((rows, attn_w + pool_width), BF16),
        ],
        compiler_params=pltpu.CompilerParams(
            dimension_semantics=("arbitrary", "arbitrary"), vmem_limit_bytes=48 * 1024 * 1024),
    )(_t5_bucket_table(n_buckets), rel_bias * LOG2E, sinks * LOG2E,
      x2, gain.reshape(1, d), w_in.astype(BF16),
      q_gain.reshape(1, attn_w), jnp.tile(k_norm, n_kv).reshape(1, kv_w),
      mean2, pool_w.astype(BF16), pool_scale.reshape(1, pool_width), w_out.astype(BF16))


def kernel(x, p, ffn1_norm, ffn1_w_gu, ffn1_w_down, mix_norm, w_in, q_norm, k_norm, rel_bias,
           sinks, pool_w, pool_scale, w_out, ffn2_norm, ffn2_w_gu, ffn2_w_down, ple_norm,
           ple_w_gate, ple_b_gate, ple_w_proj, ple_post_norm):
    batch, seq, d = x.shape
    x2 = x.reshape(batch * seq, d)
    for i in range(p.shape[0]):
        x2 = _ffn(x2, ffn1_norm[i], ffn1_w_gu[i], ffn1_w_down[i])
        x2 = _mixer(x2, seq, mix_norm[i], w_in[i], q_norm[i], k_norm[i], rel_bias, sinks[i],
                    pool_w[i], pool_scale[i], w_out[i])
        x2 = _ffn(x2, ffn2_norm[i], ffn2_w_gu[i], ffn2_w_down[i],
                  ple=(p[i].reshape(batch * seq, -1), ple_norm[i], ple_w_gate[i], ple_b_gate[i],
                       ple_w_proj[i], ple_post_norm[i]))
    return x2.reshape(batch, seq, d)
```

```python
import functools
import math

import jax
import jax.numpy as jnp
from jax import lax
from jax.experimental import pallas as pl
from jax.experimental.pallas import tpu as pltpu

F32 = jnp.float32
BF16 = jnp.bfloat16

WINDOW = 128
MAX_DISTANCE = 128
POOL_WINDOWS = (2, 4, 8, 16)
MACARON_WEIGHT = 0.5
EPS = 1e-6
NEG_INF = -1e30
LOG2E = math.log2(math.e)

V7X_LANES = 128
V7X_MXU_DIM = 256

FFN_ROWS = 512
MIX_ROWS = 512
MIX_SEQS = 2
MAX_POOL = max(POOL_WINDOWS)


def _rms(x, gain):
    ms = jnp.mean(x * x, axis=-1, keepdims=True)
    return x * lax.rsqrt(ms + EPS) * gain


def _const_spec(shape):
    zeros = (0,) * len(shape)
    return pl.BlockSpec(shape, lambda *_: zeros, pipeline_mode=pl.Buffered(1))


def _ffn_kernel(x_ref, gain_ref, wgu_ref, wd_ref, *rest, d_ff, chunk, with_ple):
    if with_ple:
        (p_ref, pgain_ref, wgate_ref, bgate_ref, wproj_ref, post_ref, o_ref,
         h_ref, acc_ref, x1_ref, hp_ref) = rest

        @pl.when(pl.program_id(0) == 0)
        def _zero_carry():
            x1_ref[...] = jnp.zeros_like(x1_ref)

        o_ref[...] = jnp.dot(p_ref[...].astype(BF16), wproj_ref[...], preferred_element_type=F32)
    else:
        o_ref, h_ref, acc_ref = rest
    h_ref[...] = _rms(x_ref[...], gain_ref[...]).astype(BF16)

    n_chunks = d_ff // chunk
    n_parts = x_ref.shape[1] // (2 * chunk)
    assert 3 * n_parts < n_chunks
    for c in range(n_chunks):
        if with_ple and c == 1:
            hp_ref[...] = _rms(x1_ref[...], pgain_ref[...]).astype(BF16)
            o_ref[...] = _rms(o_ref[...], post_ref[...])
        if with_ple and c >= 3 and c % 3 == 0 and c // 3 <= n_parts:
            part = slice((c // 3 - 1) * 2 * chunk, (c // 3) * 2 * chunk)
            gate = jax.nn.sigmoid(
                jnp.dot(hp_ref[...], wgate_ref[:, part], preferred_element_type=F32)
                + bgate_ref[:, part])
            o_ref[:, part] = x1_ref[:, part] + gate * o_ref[:, part]
        cols = slice(c * chunk, (c + 1) * chunk)
        ucols = slice(d_ff + c * chunk, d_ff + (c + 1) * chunk)
        g = jnp.dot(h_ref[...], wgu_ref[:, cols], preferred_element_type=F32)
        u = jnp.dot(h_ref[...], wgu_ref[:, ucols], preferred_element_type=F32)
        act = (g * jax.nn.sigmoid(g)) * u
        down = jnp.dot(act.astype(BF16), wd_ref[cols, :], preferred_element_type=F32)
        if c == 0:
            acc_ref[...] = down
        elif c < n_chunks - 1:
            acc_ref[...] += down
        else:
            x1 = x_ref[...] + MACARON_WEIGHT * (acc_ref[...] + down)
    if with_ple:
        x1_ref[...] = x1
    else:
        o_ref[...] = x1


def _ffn(x2, gain, w_gu, w_down, ple=None):
    n_tok, d = x2.shape
    d_ff = w_down.shape[0]
    chunk = V7X_MXU_DIM
    assert d_ff % chunk == 0 and n_tok % FFN_ROWS == 0 and w_gu.shape == (d, 2 * d_ff)
    n_tiles = n_tok // FFN_ROWS
    scratch = [pltpu.VMEM((FFN_ROWS, d), BF16), pltpu.VMEM((FFN_ROWS, d), F32)]
    args = [x2, gain.reshape(1, d), w_gu.astype(BF16), w_down.astype(BF16)]
    weight_specs = [_const_spec((1, d)), _const_spec(w_gu.shape), _const_spec(w_down.shape)]
    if ple is None:
        grid = (n_tiles,)
        out_spec = pl.BlockSpec((FFN_ROWS, d), lambda s: (s, 0))
        in_specs = [out_spec] + weight_specs
    else:
        p2, pgain, w_gate, b_gate, w_proj, post_gain = ple
        grid = (n_tiles + 1,)
        ffn_tile = lambda s: (jnp.minimum(s, n_tiles - 1), 0)
        ple_tile = lambda s: (jnp.maximum(s - 1, 0), 0)
        out_spec = pl.BlockSpec((FFN_ROWS, d), ple_tile)
        in_specs = [pl.BlockSpec((FFN_ROWS, d), ffn_tile)] + weight_specs + [
            pl.BlockSpec((FFN_ROWS, p2.shape[1]), ple_tile), _const_spec((1, d)),
            _const_spec(w_gate.shape), _const_spec((1, d)), _const_spec(w_proj.shape),
            _const_spec((1, d))]
        args += [p2, pgain.reshape(1, d), w_gate.astype(BF16), b_gate.reshape(1, d),
                 w_proj.astype(BF16), post_gain.reshape(1, d)]
        scratch.append(pltpu.VMEM((FFN_ROWS, d), F32))
        scratch.append(pltpu.VMEM((FFN_ROWS, d), BF16))
    return pl.pallas_call(
        functools.partial(_ffn_kernel, d_ff=d_ff, chunk=chunk, with_ple=ple is not None),
        name="ffn_ple" if ple is not None else "ffn",
        grid=grid,
        in_specs=in_specs,
        out_specs=out_spec,
        out_shape=jax.ShapeDtypeStruct((n_tok, d), F32),
        scratch_shapes=scratch,
        compiler_params=pltpu.CompilerParams(
            dimension_semantics=("arbitrary",), vmem_limit_bytes=48 * 1024 * 1024),
    )(*args)


def _head_rms_cols(t, mean2, gain):
    cols = []
    for c in range(t.shape[1] // V7X_LANES):
        sl = slice(c * V7X_LANES, (c + 1) * V7X_LANES)
        tc = t[:, sl]
        sq = tc * tc
        hi = sq.astype(BF16)
        lo = (sq - hi.astype(F32)).astype(BF16)
        ms = jnp.dot(jnp.concatenate([hi, lo], axis=1), mean2, preferred_element_type=F32)
        cols.append(tc * lax.rsqrt(ms + EPS) * gain[:, sl])
    return cols


def _dup_half(t, lane, half):
    r = pltpu.roll(t, half, axis=1)
    low = lane < half
    return jnp.where(low, t, r), jnp.where(low, r, t)


def _mixer_kernel(bucket_ref, relb_ref, sinks_ref, x_ref, gain_ref, win_ref, qg_ref, kg_ref,
                  mean2_ref, poolw_ref, pscale_ref, wout_ref, o_ref,
                  tbl_ref, qs_ref, kctx_ref, vctx_ref, uext_ref, cat_ref,
                  *, rows, n_heads, n_kv, head_dim, n_buckets):
    blk = WINDOW
    n_seq = x_ref.shape[0]
    n_blk = rows // blk
    group = n_heads // n_kv
    attn_w = n_heads * head_dim
    kv_w = n_kv * head_dim
    pool_w = cat_ref.shape[-1] - attn_w
    pool_g = pool_w // len(POOL_WINDOWS)
    j = pl.program_id(1)

    @pl.when((pl.program_id(0) == 0) & (j == 0))
    def _build_bias():
        def rows8(r, carry):
            r0 = pl.multiple_of(r * 8, 8)
            bidx = bucket_ref[pl.ds(r0, 8), :]
            accs = [jnp.zeros(bidx.shape, F32) for _ in range(n_heads)]
            for b in range(n_buckets):
                hit = bidx == b
                accs = [jnp.where(hit, relb_ref[b, h], accs[h]) for h in range(n_heads)]
            for h in range(n_heads):
                tbl_ref[h, pl.ds(r0, 8), :] = accs[h]
            return carry
        lax.fori_loop(0, blk // 8, rows8, 0)

    @pl.when(j == 0)
    def _reset_carry():
        kctx_ref[:, :, 0:blk, :] = jnp.zeros((n_seq, n_kv, blk, V7X_LANES), BF16)
        vctx_ref[:, :, 0:blk, :] = jnp.zeros((n_seq, n_kv, blk, 2 * V7X_LANES), BF16)
        uext_ref[:, 0:MAX_POOL, :] = jnp.zeros((n_seq, MAX_POOL, pool_w), F32)

    lane = lax.broadcasted_iota(jnp.int32, (rows, V7X_LANES), 1)
    low = lane < head_dim
    ones = jnp.ones((rows, V7X_LANES), BF16)
    qi = lax.broadcasted_iota(jnp.int32, (blk, 2 * blk), 0)
    kj = lax.broadcasted_iota(jnp.int32, (blk, 2 * blk), 1)
    dist = qi + blk - kj
    band = (dist >= 0) & (dist < WINDOW)
    low_blk = lax.broadcasted_iota(jnp.int32, (blk, V7X_LANES), 1) < head_dim
    t_pos = j * rows + lax.broadcasted_iota(jnp.int32, (rows, 1), 0)

    projs = []
    for s in range(n_seq):
        h = _rms(x_ref[s], gain_ref[...]).astype(BF16)
        projs.append(jnp.dot(h, win_ref[...], preferred_element_type=F32))

    for s, proj in enumerate(projs):
        q = proj[:, 0:attn_w]
        k = proj[:, attn_w:attn_w + kv_w]
        v = proj[:, attn_w + kv_w:attn_w + 2 * kv_w]
        uext_ref[s, MAX_POOL:MAX_POOL + rows, :] = proj[:, attn_w + 2 * kv_w:]
        for c, qc in enumerate(_head_rms_cols(q, mean2_ref[...], qg_ref[...])):
            q_lo = jnp.where(low, qc, 0.0).astype(BF16)
            q_hi = jnp.where(low, 0.0, qc).astype(BF16)
            kvh, slot = divmod(2 * c, group)
            for b in range(n_blk):
                qs_ref[s, kvh, b, slot * blk:(slot + 1) * blk, :] = q_lo[b * blk:(b + 1) * blk]
                qs_ref[s, kvh, b, (slot + 1) * blk:(slot + 2) * blk, :] = q_hi[b * blk:(b + 1) * blk]
        (kn,) = _head_rms_cols(k, mean2_ref[...], kg_ref[...])
        kv_pairs = zip(_dup_half(kn, lane, head_dim), _dup_half(v, lane, head_dim))
        for kvh, (kd, vd) in enumerate(kv_pairs):
            kctx_ref[s, kvh, blk:blk + rows, :] = kd.astype(BF16)
            vctx_ref[s, kvh, blk:blk + rows, 0:V7X_LANES] = vd.astype(BF16)
            vctx_ref[s, kvh, blk:blk + rows, V7X_LANES:] = ones

    def scores(s, b, kvh):
        kc = kctx_ref[s, kvh, b * blk:(b + 2) * blk, :]
        return lax.dot_general(qs_ref[s, kvh, b], kc, (((1,), (1,)), ((), ())),
                               preferred_element_type=F32)

    def attend(sc, s, b, kvh):
        r0 = b * blk
        valid = band if b > 0 else band & ((kj >= blk) | (j > 0))
        vc = vctx_ref[s, kvh, r0:r0 + 2 * blk, :]
        es, ms = [], []
        for i in range(group):
            head = kvh * group + i
            si = jnp.where(valid, sc[i * blk:(i + 1) * blk] + tbl_ref[head], NEG_INF)
            m = jnp.maximum(jnp.max(si, axis=-1, keepdims=True), sinks_ref[head])
            es.append(jnp.exp2(si - m).astype(BF16))
            ms.append(m)
        pv = jnp.dot(jnp.concatenate(es, axis=0), vc, preferred_element_type=F32)
        outs = []
        for i in range(group):
            head = kvh * group + i
            pvi = pv[i * blk:(i + 1) * blk]
            den = pvi[:, V7X_LANES:] + jnp.exp2(sinks_ref[head] - ms[i])
            outs.append(pvi[:, 0:V7X_LANES] / den)
        for pair in range(group // 2):
            col = (kvh * group + 2 * pair) * head_dim
            both = jnp.where(low_blk, outs[2 * pair], outs[2 * pair + 1])
            cat_ref[s, r0:r0 + blk, col:col + V7X_LANES] = both.astype(BF16)

    items = [(s, b, kvh) for b in range(n_blk) for kvh in range(n_kv) for s in range(n_seq)]
    sc_next = scores(*items[0])
    for n, item in enumerate(items):
        sc = sc_next
        if n + 1 < len(items):
            sc_next = scores(*items[n + 1])
        attend(sc, *item)

    for g, w in enumerate(POOL_WINDOWS):
        cols = slice(g * pool_g, (g + 1) * pool_g)
        cnt = jnp.minimum(t_pos + 1, w).astype(F32)
        for s in range(n_seq):
            total = uext_ref[s, :, cols]
            shift = 1
            while shift < w:
                total = total + pltpu.roll(total, shift, axis=0)
                shift *= 2
            d = total[MAX_POOL:] * (1.0 / cnt) - uext_ref[s, MAX_POOL:, cols]
            y = jnp.dot(d.astype(BF16), poolw_ref[g], preferred_element_type=F32)
            cat_ref[s, :, attn_w + g * pool_g:attn_w + (g + 1) * pool_g] = (
                y * pscale_ref[:, cols]).astype(BF16)

    for s in range(n_seq):
        o_ref[s] = x_ref[s] + jnp.dot(cat_ref[s], wout_ref[...], preferred_element_type=F32)
    kctx_ref[:, :, 0:blk, :] = kctx_ref[:, :, rows:rows + blk, :]
    vctx_ref[:, :, 0:blk, :] = vctx_ref[:, :, rows:rows + blk, :]
    uext_ref[:, 0:MAX_POOL, :] = uext_ref[:, rows:rows + MAX_POOL, :]


def _t5_bucket_table(n_buckets):
    qi = jnp.arange(WINDOW)[:, None]
    kj = jnp.arange(2 * WINDOW)[None, :]
    n = jnp.maximum(qi + WINDOW - kj, 0)
    max_exact = n_buckets // 2
    nf = jnp.maximum(n, 1).astype(F32)
    large = max_exact + (jnp.log(nf / max_exact) / math.log(MAX_DISTANCE / max_exact)
                         * (n_buckets - max_exact)).astype(jnp.int32)
    large = jnp.minimum(large, n_buckets - 1)
    return jnp.where(n < max_exact, n, large).astype(jnp.int32)


def _mixer(x2, seq, gain, w_in, q_norm, k_norm, rel_bias, sinks, pool_w, pool_scale, w_out):
    n_tok, d = x2.shape
    head_dim = q_norm.shape[0]
    n_buckets, n_heads = rel_bias.shape
    n_groups, pool_g, _ = pool_w.shape
    attn_w = n_heads * head_dim
    pool_width = n_groups * pool_g
    kv_w = (w_in.shape[1] - attn_w - pool_width) // 2
    n_kv = kv_w // head_dim
    group = n_heads // n_kv
    rows = MIX_ROWS
    assert n_groups == len(POOL_WINDOWS) and pool_g == V7X_LANES
    assert 2 * head_dim == V7X_LANES and kv_w == V7X_LANES and group % 2 == 0
    assert seq % rows == 0 and rows % WINDOW == 0 and w_out.shape[0] == attn_w + pool_width
    tiles = seq // rows

    assert head_dim & (head_dim - 1) == 0
    lane_head = jnp.arange(V7X_LANES) // head_dim
    mean_bd = jnp.where(lane_head[:, None] == lane_head[None, :], 1.0 / head_dim, 0.0).astype(BF16)
    mean2 = jnp.concatenate([mean_bd, mean_bd], axis=0)
    q_gain = jnp.tile(q_norm, n_heads) * (LOG2E / math.sqrt(head_dim))

    n_seq = MIX_SEQS
    batch = n_tok // seq
    assert batch % n_seq == 0
    row_spec = pl.BlockSpec((None, n_seq, rows, d), lambda b, j: (b, 0, j, 0))
    smem = pl.BlockSpec(memory_space=pltpu.SMEM)
    kernel = functools.partial(_mixer_kernel, rows=rows, n_heads=n_heads, n_kv=n_kv,
                               head_dim=head_dim, n_buckets=n_buckets)
    out = pl.pallas_call(
        kernel,
        name="mixer",
        grid=(batch // n_seq, tiles),
        in_specs=[
            _const_spec((WINDOW, 2 * WINDOW)), smem, smem,
            row_spec, _const_spec((1, d)), _const_spec(w_in.shape),
            _const_spec((1, attn_w)), _const_spec((1, kv_w)), _const_spec(mean2.shape),
            _const_spec(pool_w.shape), _const_spec((1, pool_width)), _const_spec(w_out.shape),
        ],
        out_specs=row_spec,
        out_shape=jax.ShapeDtypeStruct((batch // n_seq, n_seq, seq, d), F32),
        scratch_shapes=[
            pltpu.VMEM((n_heads, WINDOW, 2 * WINDOW), F32),
            pltpu.VMEM((n_seq, n_kv, rows // WINDOW, group * WINDOW, V7X_LANES), BF16),
            pltpu.VMEM((n_seq, n_kv, WINDOW + rows, V7X_LANES), BF16),
            pltpu.VMEM((n_seq, n_kv, WINDOW + rows, 2 * V7X_LANES), BF16),
            pltpu.VMEM((n_seq, MAX_POOL + rows, pool_width), F32),
            pltpu.VMEM((n_seq, rows, attn_w + pool_width), BF16),
        ],
        compiler_params=pltpu.CompilerParams(
            dimension_semantics=("arbitrary", "arbitrary"), vmem_limit_bytes=48 * 1024 * 1024),
    )(_t5_bucket_table(n_buckets), rel_bias * LOG2E, sinks * LOG2E,
      x2.reshape(batch // n_seq, n_seq, seq, d), gain.reshape(1, d), w_in.astype(BF16),
      q_gain.reshape(1, attn_w), jnp.tile(k_norm, n_kv).reshape(1, kv_w),
      mean2, pool_w.astype(BF16), pool_scale.reshape(1, pool_width), w_out.astype(BF16))
    return out.reshape(n_tok, d)


def kernel(x, p, ffn1_norm, ffn1_w_gu, ffn1_w_down, mix_norm, w_in, q_norm, k_norm, rel_bias,
           sinks, pool_w, pool_scale, w_out, ffn2_norm, ffn2_w_gu, ffn2_w_down, ple_norm,
           ple_w_gate, ple_b_gate, ple_w_proj, ple_post_norm):
    batch, seq, d = x.shape
    x2 = x.reshape(batch * seq, d)
    for i in range(p.shape[0]):
        x2 = _ffn(x2, ffn1_norm[i], ffn1_w_gu[i], ffn1_w_down[i])
        x2 = _mixer(x2, seq, mix_norm[i], w_in[i], q_norm[i], k_norm[i], rel_bias, sinks[i],
                    pool_w[i], pool_scale[i], w_out[i])
        x2 = _ffn(x2, ffn2_norm[i], ffn2_w_gu[i], ffn2_w_down[i],
                  ple=(p[i].reshape(batch * seq, -1), ple_norm[i], ple_w_gate[i], ple_b_gate[i],
                       ple_w_proj[i], ple_post_norm[i]))
    return x2.reshape(batch, seq, d)
```

```python
import functools
import math

import jax
import jax.numpy as jnp
from jax import lax
from jax.experimental import pallas as pl
from jax.experimental.pallas import tpu as pltpu

F32 = jnp.float32
BF16 = jnp.bfloat16

WINDOW = 128
MAX_DISTANCE = 128
POOL_WINDOWS = (2, 4, 8, 16)
MACARON_WEIGHT = 0.5
EPS = 1e-6
NEG_INF = -1e30
LOG2E = math.log2(math.e)

V7X_LANES = 128
V7X_MXU_DIM = 256

TILE_ROWS = 512
FFN_CHUNK = V7X_MXU_DIM
MAX_POOL = max(POOL_WINDOWS)
VMEM_LIMIT = 56 * 1024 * 1024


def _rms(x, gain):
    ms = jnp.mean(x * x, axis=-1, keepdims=True)
    return x * lax.rsqrt(ms + EPS) * gain


def _const_spec(shape):
    zeros = (0,) * len(shape)
    return pl.BlockSpec(shape, lambda *_: zeros, pipeline_mode=pl.Buffered(1))


def _ffn_chunks(x, h_ref, acc_ref, wgu_ref, wd_ref, between=None):
    d_ff = wd_ref.shape[0]
    n_chunks = d_ff // FFN_CHUNK
    for c in range(n_chunks):
        cols = slice(c * FFN_CHUNK, (c + 1) * FFN_CHUNK)
        ucols = slice(d_ff + c * FFN_CHUNK, d_ff + (c + 1) * FFN_CHUNK)
        g = jnp.dot(h_ref[...], wgu_ref[:, cols], preferred_element_type=F32)
        u = jnp.dot(h_ref[...], wgu_ref[:, ucols], preferred_element_type=F32)
        act = (g * jax.nn.sigmoid(g)) * u
        down = jnp.dot(act.astype(BF16), wd_ref[cols, :], preferred_element_type=F32)
        if c == 0:
            acc_ref[...] = down
        elif c < n_chunks - 1:
            acc_ref[...] += down
        if between is not None:
            between(c)
    return x + MACARON_WEIGHT * (acc_ref[...] + down)


def _ffn_ple_kernel(x_ref, gain_ref, wgu_ref, wd_ref, p_ref, pgain_ref, wgate_ref, bgate_ref,
                    wproj_ref, post_ref, o_ref, h_ref, acc_ref):
    h_ref[...] = _rms(x_ref[...], gain_ref[...]).astype(BF16)
    x1 = _ffn_chunks(x_ref[...], h_ref, acc_ref, wgu_ref, wd_ref)
    hp = _rms(x1, pgain_ref[...]).astype(BF16)
    gate = jax.nn.sigmoid(jnp.dot(hp, wgate_ref[...], preferred_element_type=F32) + bgate_ref[...])
    e = jnp.dot(p_ref[...].astype(BF16), wproj_ref[...], preferred_element_type=F32)
    o_ref[...] = x1 + gate * _rms(e, post_ref[...])


def _ffn_ple(x2, gain, w_gu, w_down, p2, pgain, w_gate, b_gate, w_proj, post_gain):
    n_tok, d = x2.shape
    d_ff = w_down.shape[0]
    assert d_ff % FFN_CHUNK == 0 and n_tok % TILE_ROWS == 0 and w_gu.shape == (d, 2 * d_ff)
    row_spec = pl.BlockSpec((TILE_ROWS, d), lambda s: (s, 0))
    return pl.pallas_call(
        _ffn_ple_kernel,
        name="ffn_ple",
        grid=(n_tok // TILE_ROWS,),
        in_specs=[row_spec, _const_spec((1, d)), _const_spec(w_gu.shape), _const_spec(w_down.shape),
                  pl.BlockSpec((TILE_ROWS, p2.shape[1]), lambda s: (s, 0)), _const_spec((1, d)),
                  _const_spec(w_gate.shape), _const_spec((1, d)), _const_spec(w_proj.shape),
                  _const_spec((1, d))],
        out_specs=row_spec,
        out_shape=jax.ShapeDtypeStruct((n_tok, d), F32),
        scratch_shapes=[pltpu.VMEM((TILE_ROWS, d), BF16), pltpu.VMEM((TILE_ROWS, d), F32)],
        compiler_params=pltpu.CompilerParams(
            dimension_semantics=("arbitrary",), vmem_limit_bytes=VMEM_LIMIT),
    )(x2, gain.reshape(1, d), w_gu.astype(BF16), w_down.astype(BF16), p2, pgain.reshape(1, d),
      w_gate.astype(BF16), b_gate.reshape(1, d), w_proj.astype(BF16), post_gain.reshape(1, d))


def _head_rms_cols(t, mean2, gain):
    cols = []
    for c in range(t.shape[1] // V7X_LANES):
        sl = slice(c * V7X_LANES, (c + 1) * V7X_LANES)
        tc = t[:, sl]
        sq = tc * tc
        hi = sq.astype(BF16)
        lo = (sq - hi.astype(F32)).astype(BF16)
        ms = jnp.dot(jnp.concatenate([hi, lo], axis=1), mean2, preferred_element_type=F32)
        cols.append(tc * lax.rsqrt(ms + EPS) * gain[:, sl])
    return cols


def _dup_half(t, lane, half):
    r = pltpu.roll(t, half, axis=1)
    low = lane < half
    return jnp.where(low, t, r), jnp.where(low, r, t)


def _build_bias_table(bucket_ref, relb_ref, tbl_ref):
    n_buckets, n_heads = relb_ref.shape

    def rows8(r, carry):
        r0 = pl.multiple_of(r * 8, 8)
        bidx = bucket_ref[pl.ds(r0, 8), :]
        accs = [jnp.zeros(bidx.shape, F32) for _ in range(n_heads)]
        for b in range(n_buckets):
            hit = bidx == b
            accs = [jnp.where(hit, relb_ref[b, h], accs[h]) for h in range(n_heads)]
        for h in range(n_heads):
            tbl_ref[h, pl.ds(r0, 8), :] = accs[h]
        return carry

    lax.fori_loop(0, WINDOW // 8, rows8, 0)


def _mixer_pieces(j, x_ref, o_ref, sinks_ref, gain_ref, win_ref, qg_ref, kg_ref, mean2_ref,
                  poolw_ref, pscale_ref, wout_ref, tbl_ref, qs_ref, kctx_ref, vctx_ref,
                  uext_ref, cat_ref):
    blk = WINDOW
    rows = x_ref.shape[0]
    n_blk = rows // blk
    n_heads = tbl_ref.shape[0]
    n_kv = kctx_ref.shape[0]
    group = n_heads // n_kv
    head_dim = V7X_LANES // 2
    attn_w = n_heads * head_dim
    kv_w = n_kv * head_dim
    pool_w = cat_ref.shape[1] - attn_w
    pool_g = pool_w // len(POOL_WINDOWS)

    lane = lax.broadcasted_iota(jnp.int32, (rows, V7X_LANES), 1)
    low = lane < head_dim

    h = _rms(x_ref[...], gain_ref[...]).astype(BF16)
    q = jnp.dot(h, win_ref[:, 0:attn_w], preferred_element_type=F32)
    yield
    kvu = jnp.dot(h, win_ref[:, attn_w:], preferred_element_type=F32)
    k = kvu[:, 0:kv_w]
    v = kvu[:, kv_w:2 * kv_w]
    uext_ref[MAX_POOL:MAX_POOL + rows, :] = kvu[:, 2 * kv_w:]
    yield
    for c, qc in enumerate(_head_rms_cols(q, mean2_ref[...], qg_ref[...])):
        q_lo = jnp.where(low, qc, 0.0).astype(BF16)
        q_hi = jnp.where(low, 0.0, qc).astype(BF16)
        kvh, slot = divmod(2 * c, group)
        for b in range(n_blk):
            qs_ref[kvh, b, slot * blk:(slot + 1) * blk, :] = q_lo[b * blk:(b + 1) * blk]
            qs_ref[kvh, b, (slot + 1) * blk:(slot + 2) * blk, :] = q_hi[b * blk:(b + 1) * blk]
    yield
    (kn,) = _head_rms_cols(k, mean2_ref[...], kg_ref[...])
    ones = jnp.ones((rows, V7X_LANES), BF16)
    kv_pairs = zip(_dup_half(kn, lane, head_dim), _dup_half(v, lane, head_dim))
    for kvh, (kd, vd) in enumerate(kv_pairs):
        kctx_ref[kvh, blk:blk + rows, :] = kd.astype(BF16)
        vctx_ref[kvh, blk:blk + rows, 0:V7X_LANES] = vd.astype(BF16)
        vctx_ref[kvh, blk:blk + rows, V7X_LANES:] = ones
    yield

    qi = lax.broadcasted_iota(jnp.int32, (blk, 2 * blk), 0)
    kj = lax.broadcasted_iota(jnp.int32, (blk, 2 * blk), 1)
    dist = qi + blk - kj
    band = (dist >= 0) & (dist < WINDOW)
    low_blk = lax.broadcasted_iota(jnp.int32, (blk, V7X_LANES), 1) < head_dim

    def scores(b, kvh):
        kc = kctx_ref[kvh, b * blk:(b + 2) * blk, :]
        return lax.dot_general(qs_ref[kvh, b], kc, (((1,), (1,)), ((), ())),
                               preferred_element_type=F32)

    def attend(sc, b, kvh):
        r0 = b * blk
        valid = band if b > 0 else band & ((kj >= blk) | (j > 0))
        vc = vctx_ref[kvh, r0:r0 + 2 * blk, :]
        es, ms = [], []
        for i in range(group):
            head = kvh * group + i
            si = jnp.where(valid, sc[i * blk:(i + 1) * blk] + tbl_ref[head], NEG_INF)
            m = jnp.maximum(jnp.max(si, axis=-1, keepdims=True), sinks_ref[head])
            es.append(jnp.exp2(si - m).astype(BF16))
            ms.append(m)
        pv = jnp.dot(jnp.concatenate(es, axis=0), vc, preferred_element_type=F32)
        outs = []
        for i in range(group):
            head = kvh * group + i
            pvi = pv[i * blk:(i + 1) * blk]
            den = pvi[:, V7X_LANES:] + jnp.exp2(sinks_ref[head] - ms[i])
            outs.append(pvi[:, 0:V7X_LANES] / den)
        for pair in range(group // 2):
            col = (kvh * group + 2 * pair) * head_dim
            both = jnp.where(low_blk, outs[2 * pair], outs[2 * pair + 1])
            cat_ref[r0:r0 + blk, col:col + V7X_LANES] = both.astype(BF16)

    items = [(b, kvh) for b in range(n_blk) for kvh in range(n_kv)]
    sc_next = scores(*items[0])
    for n, item in enumerate(items):
        sc = sc_next
        if n + 1 < len(items):
            sc_next = scores(*items[n + 1])
        attend(sc, *item)
        yield

    t_pos = j * rows + lax.broadcasted_iota(jnp.int32, (rows, 1), 0)
    for g, w in enumerate(POOL_WINDOWS):
        cols = slice(g * pool_g, (g + 1) * pool_g)
        total = uext_ref[:, cols]
        shift = 1
        while shift < w:
            total = total + pltpu.roll(total, shift, axis=0)
            shift *= 2
        cnt = jnp.minimum(t_pos + 1, w).astype(F32)
        d = total[MAX_POOL:] * (1.0 / cnt) - uext_ref[MAX_POOL:, cols]
        y = jnp.dot(d.astype(BF16), poolw_ref[g], preferred_element_type=F32)
        cat_ref[:, attn_w + g * pool_g:attn_w + (g + 1) * pool_g] = (
            y * pscale_ref[:, cols]).astype(BF16)
        if g % 2 == 1:
            yield

    half_d = x_ref.shape[1] // 2
    for c0 in (0, half_d):
        cols = slice(c0, c0 + half_d)
        o_ref[:, cols] = x_ref[:, cols] + jnp.dot(
            cat_ref[...], wout_ref[:, cols], preferred_element_type=F32)
        if c0 == 0:
            yield
    kctx_ref[:, 0:blk, :] = kctx_ref[:, rows:rows + blk, :]
    vctx_ref[:, 0:blk, :] = vctx_ref[:, rows:rows + blk, :]
    uext_ref[0:MAX_POOL, :] = uext_ref[rows:rows + MAX_POOL, :]
    yield


def _ffn_mixer_kernel(bucket_ref, relb_ref, sinks_ref, x_ref, fgain_ref, wgu_ref, wd_ref,
                      mgain_ref, win_ref, qg_ref, kg_ref, mean2_ref, poolw_ref, pscale_ref,
                      wout_ref, o_ref, h_ref, acc_ref, x1_ref, tbl_ref, qs_ref, kctx_ref,
                      vctx_ref, uext_ref, cat_ref, *, seq_tiles, pieces_after):
    s = pl.program_id(0)
    j = lax.rem(jnp.maximum(s - 1, 0), seq_tiles)

    @pl.when(s == 0)
    def _first_step():
        _build_bias_table(bucket_ref, relb_ref, tbl_ref)
        x1_ref[...] = jnp.zeros_like(x1_ref)

    @pl.when(j == 0)
    def _reset_carry():
        kctx_ref[:, 0:WINDOW, :] = jnp.zeros((kctx_ref.shape[0], WINDOW, kctx_ref.shape[2]), BF16)
        vctx_ref[:, 0:WINDOW, :] = jnp.zeros((vctx_ref.shape[0], WINDOW, vctx_ref.shape[2]), BF16)
        uext_ref[0:MAX_POOL, :] = jnp.zeros((MAX_POOL, uext_ref.shape[1]), F32)

    mixer = _mixer_pieces(j, x1_ref, o_ref, sinks_ref, mgain_ref, win_ref, qg_ref, kg_ref,
                          mean2_ref, poolw_ref, pscale_ref, wout_ref, tbl_ref, qs_ref,
                          kctx_ref, vctx_ref, uext_ref, cat_ref)

    def between(c):
        for _ in range(pieces_after[c]):
            next(mixer)

    h_ref[...] = _rms(x_ref[...], fgain_ref[...]).astype(BF16)
    x1 = _ffn_chunks(x_ref[...], h_ref, acc_ref, wgu_ref, wd_ref, between)
    x1_ref[...] = x1


PIECES_AFTER_CHUNK = (2, 2, 1, 1, 1, 1, 1, 1, 1, 1, 4)


def _t5_bucket_table(n_buckets):
    qi = jnp.arange(WINDOW)[:, None]
    kj = jnp.arange(2 * WINDOW)[None, :]
    n = jnp.maximum(qi + WINDOW - kj, 0)
    max_exact = n_buckets // 2
    nf = jnp.maximum(n, 1).astype(F32)
    large = max_exact + (jnp.log(nf / max_exact) / math.log(MAX_DISTANCE / max_exact)
                         * (n_buckets - max_exact)).astype(jnp.int32)
    large = jnp.minimum(large, n_buckets - 1)
    return jnp.where(n < max_exact, n, large).astype(jnp.int32)


def _ffn_mixer(x2, seq, fgain, w_gu, w_down, mgain, w_in, q_norm, k_norm, rel_bias, sinks,
               pool_w, pool_scale, w_out):
    n_tok, d = x2.shape
    d_ff = w_down.shape[0]
    head_dim = q_norm.shape[0]
    n_buckets, n_heads = rel_bias.shape
    n_groups, pool_g, _ = pool_w.shape
    attn_w = n_heads * head_dim
    pool_width = n_groups * pool_g
    kv_w = (w_in.shape[1] - attn_w - pool_width) // 2
    n_kv = kv_w // head_dim
    group = n_heads // n_kv
    rows = TILE_ROWS
    n_tiles = n_tok // rows
    assert d_ff % FFN_CHUNK == 0 and w_gu.shape == (d, 2 * d_ff)
    assert n_groups == len(POOL_WINDOWS) and pool_g == V7X_LANES
    assert 2 * head_dim == V7X_LANES and kv_w == V7X_LANES and group % 2 == 0
    assert seq % rows == 0 and rows % WINDOW == 0 and w_out.shape[0] == attn_w + pool_width
    n_pieces = 4 + (rows // WINDOW) * n_kv + 4
    assert len(PIECES_AFTER_CHUNK) == d_ff // FFN_CHUNK and sum(PIECES_AFTER_CHUNK) == n_pieces

    assert head_dim & (head_dim - 1) == 0
    lane_head = jnp.arange(V7X_LANES) // head_dim
    mean_bd = jnp.where(lane_head[:, None] == lane_head[None, :], 1.0 / head_dim, 0.0).astype(BF16)
    mean2 = jnp.concatenate([mean_bd, mean_bd], axis=0)
    q_gain = jnp.tile(q_norm, n_heads) * (LOG2E / math.sqrt(head_dim))

    smem = pl.BlockSpec(memory_space=pltpu.SMEM)
    kernel = functools.partial(_ffn_mixer_kernel, seq_tiles=seq // rows,
                               pieces_after=PIECES_AFTER_CHUNK)
    return pl.pallas_call(
        kernel,
        name="ffn_mixer",
        grid=(n_tiles + 1,),
        in_specs=[
            _const_spec((WINDOW, 2 * WINDOW)), smem, smem,
            pl.BlockSpec((rows, d), lambda s: (jnp.minimum(s, n_tiles - 1), 0)),
            _const_spec((1, d)), _const_spec(w_gu.shape), _const_spec(w_down.shape),
            _const_spec((1, d)), _const_spec(w_in.shape),
            _const_spec((1, attn_w)), _const_spec((1, kv_w)), _const_spec(mean2.shape),
            _const_spec(pool_w.shape), _const_spec((1, pool_width)), _const_spec(w_out.shape),
        ],
        out_specs=pl.BlockSpec((rows, d), lambda s: (jnp.maximum(s - 1, 0), 0)),
        out_shape=jax.ShapeDtypeStruct((n_tok, d), F32),
        scratch_shapes=[
            pltpu.VMEM((rows, d), BF16),
            pltpu.VMEM((rows, d), F32),
            pltpu.VMEM((rows, d), F32),
            pltpu.VMEM((n_heads, WINDOW, 2 * WINDOW), F32),
            pltpu.VMEM((n_kv, rows // WINDOW, group * WINDOW, V7X_LANES), BF16),
            pltpu.VMEM((n_kv, WINDOW + rows, V7X_LANES), BF16),
            pltpu.VMEM((n_kv, WINDOW + rows, 2 * V7X_LANES), BF16),
            pltpu.VMEM((MAX_POOL + rows, pool_width), F32),
            pltpu.VMEM((rows, attn_w + pool_width), BF16),
        ],
        compiler_params=pltpu.CompilerParams(
            dimension_semantics=("arbitrary",), vmem_limit_bytes=VMEM_LIMIT),
    )(_t5_bucket_table(n_buckets), rel_bias * LOG2E, sinks * LOG2E,
      x2, fgain.reshape(1, d), w_gu.astype(BF16), w_down.astype(BF16),
      mgain.reshape(1, d), w_in.astype(BF16),
      q_gain.reshape(1, attn_w), jnp.tile(k_norm, n_kv).reshape(1, kv_w),
      mean2, pool_w.astype(BF16), pool_scale.reshape(1, pool_width), w_out.astype(BF16))


def kernel(x, p, ffn1_norm, ffn1_w_gu, ffn1_w_down, mix_norm, w_in, q_norm, k_norm, rel_bias,
           sinks, pool_w, pool_scale, w_out, ffn2_norm, ffn2_w_gu, ffn2_w_down, ple_norm,
           ple_w_gate, ple_b_gate, ple_w_proj, ple_post_norm):
    batch, seq, d = x.shape
    x2 = x.reshape(batch * seq, d)
    for i in range(p.shape[0]):
        x2 = _ffn_mixer(x2, seq, ffn1_norm[i], ffn1_w_gu[i], ffn1_w_down[i], mix_norm[i], w_in[i],
                        q_norm[i], k_norm[i], rel_bias, sinks[i], pool_w[i], pool_scale[i], w_out[i])
        x2 = _ffn_ple(x2, ffn2_norm[i], ffn2_w_gu[i], ffn2_w_down[i], p[i].reshape(batch * seq, -1),
                      ple_norm[i], ple_w_gate[i], ple_b_gate[i], ple_w_proj[i], ple_post_norm[i])
    return x2.reshape(batch, seq, d)
```

```python
import math

import jax
import jax.numpy as jnp
from jax import lax
from jax.experimental import pallas as pl
from jax.experimental.pallas import tpu as pltpu

F32 = jnp.float32
BF16 = jnp.bfloat16

WINDOW = 128
MAX_DISTANCE = 128
POOL_WINDOWS = (2, 4, 8, 16)
MACARON_WEIGHT = 0.5
EPS = 1e-6
NEG_INF = -1e30
LOG2E = math.log2(math.e)

V7X_LANES = 128
V7X_MXU_DIM = 256
BF16_SUBLANES = 16

TILE_ROWS = 512
MIX_SEQS = 2
FFN_CHUNK = V7X_MXU_DIM
MAX_POOL = max(POOL_WINDOWS)
VMEM_LIMIT = 48 * 1024 * 1024


def _rms(x, gain):
    ms = jnp.mean(x * x, axis=-1, keepdims=True)
    return x * lax.rsqrt(ms + EPS) * gain


def _const_spec(shape):
    zeros = (0,) * len(shape)
    return pl.BlockSpec(shape, lambda *_: zeros, pipeline_mode=pl.Buffered(1))


def _ffn_half_step(x_ref, gain_ref, wgu_ref, wd_ref, h_ref, acc_ref):
    h_ref[...] = _rms(x_ref[...], gain_ref[...]).astype(BF16)
    d_ff = wd_ref.shape[0]
    n_chunks = d_ff // FFN_CHUNK
    for c in range(n_chunks):
        cols = slice(c * FFN_CHUNK, (c + 1) * FFN_CHUNK)
        ucols = slice(d_ff + c * FFN_CHUNK, d_ff + (c + 1) * FFN_CHUNK)
        g = jnp.dot(h_ref[...], wgu_ref[:, cols], preferred_element_type=F32)
        u = jnp.dot(h_ref[...], wgu_ref[:, ucols], preferred_element_type=F32)
        act = (g * jax.nn.sigmoid(g)) * u
        down = jnp.dot(act.astype(BF16), wd_ref[cols, :], preferred_element_type=F32)
        if c == 0:
            acc_ref[...] = down
        elif c < n_chunks - 1:
            acc_ref[...] += down
    return x_ref[...] + MACARON_WEIGHT * (acc_ref[...] + down)


def _ffn_kernel(x_ref, gain_ref, wgu_ref, wd_ref, o_ref, h_ref, acc_ref):
    o_ref[...] = _ffn_half_step(x_ref, gain_ref, wgu_ref, wd_ref, h_ref, acc_ref)


def _ffn_ple_kernel(x_ref, gain_ref, wgu_ref, wd_ref, p_ref, pgain_ref, wgate_ref, bgate_ref,
                    wproj_ref, post_ref, o_ref, h_ref, acc_ref):
    x1 = _ffn_half_step(x_ref, gain_ref, wgu_ref, wd_ref, h_ref, acc_ref)
    hp = _rms(x1, pgain_ref[...]).astype(BF16)
    gate = jax.nn.sigmoid(jnp.dot(hp, wgate_ref[...], preferred_element_type=F32) + bgate_ref[...])
    e = jnp.dot(p_ref[...].astype(BF16), wproj_ref[...], preferred_element_type=F32)
    o_ref[...] = x1 + gate * _rms(e, post_ref[...])


def _ffn(x2, gain, w_gu, w_down, ple=None):
    n_tok, d = x2.shape
    d_ff = w_down.shape[0]
    assert d_ff % FFN_CHUNK == 0 and n_tok % TILE_ROWS == 0 and w_gu.shape == (d, 2 * d_ff)
    row_spec = pl.BlockSpec((TILE_ROWS, d), lambda s: (s, 0))
    in_specs = [row_spec, _const_spec((1, d)), _const_spec(w_gu.shape), _const_spec(w_down.shape)]
    args = [x2, gain.reshape(1, d), w_gu.astype(BF16), w_down.astype(BF16)]
    if ple is not None:
        p2, pgain, w_gate, b_gate, w_proj, post_gain = ple
        in_specs += [pl.BlockSpec((TILE_ROWS, p2.shape[1]), lambda s: (s, 0)), _const_spec((1, d)),
                     _const_spec(w_gate.shape), _const_spec((1, d)), _const_spec(w_proj.shape),
                     _const_spec((1, d))]
        args += [p2, pgain.reshape(1, d), w_gate.astype(BF16), b_gate.reshape(1, d),
                 w_proj.astype(BF16), post_gain.reshape(1, d)]
    return pl.pallas_call(
        _ffn_kernel if ple is None else _ffn_ple_kernel,
        name="ffn" if ple is None else "ffn_ple",
        grid=(n_tok // TILE_ROWS,),
        in_specs=in_specs,
        out_specs=row_spec,
        out_shape=jax.ShapeDtypeStruct((n_tok, d), F32),
        scratch_shapes=[pltpu.VMEM((TILE_ROWS, d), BF16), pltpu.VMEM((TILE_ROWS, d), F32)],
        compiler_params=pltpu.CompilerParams(
            dimension_semantics=("arbitrary",), vmem_limit_bytes=VMEM_LIMIT),
    )(*args)


def _head_rms_cols(t, mean2, gain):
    cols = []
    for c in range(t.shape[1] // V7X_LANES):
        sl = slice(c * V7X_LANES, (c + 1) * V7X_LANES)
        tc = t[:, sl]
        sq = tc * tc
        hi = sq.astype(BF16)
        lo = (sq - hi.astype(F32)).astype(BF16)
        ms = jnp.dot(jnp.concatenate([hi, lo], axis=1), mean2, preferred_element_type=F32)
        cols.append(tc * lax.rsqrt(ms + EPS) * gain[:, sl])
    return cols


def _dup_half(t, lane, half):
    r = pltpu.roll(t, half, axis=1)
    low = lane < half
    return jnp.where(low, t, r), jnp.where(low, r, t)


def _build_score_tables(bucket_ref, relb_ref, sinks_ref, tbl_ref, fill_ref):
    n_buckets, n_heads = relb_ref.shape

    def rows8(r, carry):
        r0 = pl.multiple_of(r * 8, 8)
        bidx = bucket_ref[pl.ds(r0, 8), :]
        slot0 = lax.broadcasted_iota(jnp.int32, bidx.shape, 1) == 0
        accs = [jnp.zeros(bidx.shape, F32) for _ in range(n_heads)]
        for b in range(n_buckets):
            hit = bidx == b
            accs = [jnp.where(hit, relb_ref[b, h], accs[h]) for h in range(n_heads)]
        for h in range(n_heads):
            tbl_ref[h, pl.ds(r0, 8), :] = accs[h]
            fill_ref[h, pl.ds(r0, 8), :] = jnp.where(slot0, sinks_ref[h], NEG_INF)
        return carry

    lax.fori_loop(0, WINDOW // 8, rows8, 0)


def _mixer_kernel(bucket_ref, relb_ref, sinks_ref, x_ref, gain_ref, win_ref, qg_ref, kg_ref,
                  mean2_ref, poolw_ref, pscale_ref, wout_ref, o_ref,
                  tbl_ref, fill_ref, qs_ref, kctx_ref, vctx_ref, uext_ref, cat_ref):
    blk = WINDOW
    n_seq, rows, _ = x_ref.shape
    n_blk = rows // blk
    n_heads = tbl_ref.shape[0]
    n_kv = kctx_ref.shape[1]
    group = n_heads // n_kv
    head_dim = V7X_LANES // 2
    attn_w = n_heads * head_dim
    kv_w = n_kv * head_dim
    pool_w = cat_ref.shape[-1] - attn_w
    pool_g = pool_w // len(POOL_WINDOWS)
    j = pl.program_id(1)

    @pl.when((pl.program_id(0) == 0) & (j == 0))
    def _first_step():
        _build_score_tables(bucket_ref, relb_ref, sinks_ref, tbl_ref, fill_ref)

    @pl.when(j == 0)
    def _reset_carry():
        kctx_ref[:, :, 0:blk, :] = jnp.zeros((n_seq, n_kv, blk, V7X_LANES), BF16)
        vctx_ref[:, :, 0:blk, :] = jnp.zeros((n_seq, n_kv, blk, 2 * V7X_LANES), BF16)
        uext_ref[:, 0:MAX_POOL, :] = jnp.zeros((n_seq, MAX_POOL, pool_w), F32)

    lane = lax.broadcasted_iota(jnp.int32, (rows, V7X_LANES), 1)
    low = lane < head_dim
    ones = jnp.ones((rows, V7X_LANES), BF16)
    qi = lax.broadcasted_iota(jnp.int32, (blk, 2 * blk), 0)
    kj = lax.broadcasted_iota(jnp.int32, (blk, 2 * blk), 1)
    dist = qi + blk - kj
    band = (dist >= 0) & (dist < WINDOW)
    low_blk = lax.broadcasted_iota(jnp.int32, (blk, V7X_LANES), 1) < head_dim
    t_pos = j * rows + lax.broadcasted_iota(jnp.int32, (rows, 1), 0)
    top_row = lax.broadcasted_iota(jnp.int32, (BF16_SUBLANES, 2 * V7X_LANES), 0) == 0
    sink_row = (lax.broadcasted_iota(jnp.int32, (BF16_SUBLANES, 2 * V7X_LANES), 1)
                >= V7X_LANES).astype(F32)

    projs = []
    for s in range(n_seq):
        h = _rms(x_ref[s], gain_ref[...]).astype(BF16)
        projs.append(jnp.dot(h, win_ref[...], preferred_element_type=F32))

    for s, proj in enumerate(projs):
        q = proj[:, 0:attn_w]
        k = proj[:, attn_w:attn_w + kv_w]
        v = proj[:, attn_w + kv_w:attn_w + 2 * kv_w]
        uext_ref[s, MAX_POOL:MAX_POOL + rows, :] = proj[:, attn_w + 2 * kv_w:]
        for c, qc in enumerate(_head_rms_cols(q, mean2_ref[...], qg_ref[...])):
            q_lo = jnp.where(low, qc, 0.0).astype(BF16)
            q_hi = jnp.where(low, 0.0, qc).astype(BF16)
            kvh, slot = divmod(2 * c, group)
            for b in range(n_blk):
                qs_ref[s, kvh, b, slot * blk:(slot + 1) * blk, :] = q_lo[b * blk:(b + 1) * blk]
                qs_ref[s, kvh, b, (slot + 1) * blk:(slot + 2) * blk, :] = q_hi[b * blk:(b + 1) * blk]
        (kn,) = _head_rms_cols(k, mean2_ref[...], kg_ref[...])
        kv_pairs = zip(_dup_half(kn, lane, head_dim), _dup_half(v, lane, head_dim))
        for kvh, (kd, vd) in enumerate(kv_pairs):
            kctx_ref[s, kvh, blk:blk + rows, :] = kd.astype(BF16)
            vctx_ref[s, kvh, blk:blk + rows, 0:V7X_LANES] = vd.astype(BF16)
            vctx_ref[s, kvh, blk:blk + rows, V7X_LANES:] = ones

    def scores(s, b, kvh):
        kc = kctx_ref[s, kvh, b * blk:(b + 2) * blk, :]
        return lax.dot_general(qs_ref[s, kvh, b], kc, (((1,), (1,)), ((), ())),
                               preferred_element_type=F32)

    def attend(sc, s, b, kvh):
        r0 = b * blk
        valid = band if b > 0 else band & ((kj >= blk) | (j > 0))
        es = []
        for i in range(group):
            head = kvh * group + i
            si = jnp.where(valid, sc[i * blk:(i + 1) * blk] + tbl_ref[head], fill_ref[head])
            m = jnp.max(si, axis=-1, keepdims=True)
            es.append(jnp.exp2(si - m).astype(BF16))
        top = vctx_ref[s, kvh, r0:r0 + BF16_SUBLANES, :].astype(F32)
        top = jnp.where(top_row, sink_row, top).astype(BF16)
        vc = jnp.concatenate([top, vctx_ref[s, kvh, r0 + BF16_SUBLANES:r0 + 2 * blk, :]], axis=0)
        pv = jnp.dot(jnp.concatenate(es, axis=0), vc, preferred_element_type=F32)
        outs = [pv[i * blk:(i + 1) * blk, 0:V7X_LANES] / pv[i * blk:(i + 1) * blk, V7X_LANES:]
                for i in range(group)]
        for pair in range(group // 2):
            col = (kvh * group + 2 * pair) * head_dim
            both = jnp.where(low_blk, outs[2 * pair], outs[2 * pair + 1])
            cat_ref[s, r0:r0 + blk, col:col + V7X_LANES] = both.astype(BF16)

    items = [(s, b, kvh) for b in range(n_blk) for kvh in range(n_kv) for s in range(n_seq)]
    sc_next = scores(*items[0])
    for n, item in enumerate(items):
        sc = sc_next
        if n + 1 < len(items):
            sc_next = scores(*items[n + 1])
        attend(sc, *item)

    for g, w in enumerate(POOL_WINDOWS):
        cols = slice(g * pool_g, (g + 1) * pool_g)
        cnt = jnp.minimum(t_pos + 1, w).astype(F32)
        for s in range(n_seq):
            total = uext_ref[s, :, cols]
            shift = 1
            while shift < w:
                total = total + pltpu.roll(total, shift, axis=0)
                shift *= 2
            d = total[MAX_POOL:] * (1.0 / cnt) - uext_ref[s, MAX_POOL:, cols]
            y = jnp.dot(d.astype(BF16), poolw_ref[g], preferred_element_type=F32)
            cat_ref[s, :, attn_w + g * pool_g:attn_w + (g + 1) * pool_g] = (
                y * pscale_ref[:, cols]).astype(BF16)

    for s in range(n_seq):
        o_ref[s] = x_ref[s] + jnp.dot(cat_ref[s], wout_ref[...], preferred_element_type=F32)
    kctx_ref[:, :, 0:blk, :] = kctx_ref[:, :, rows:rows + blk, :]
    vctx_ref[:, :, 0:blk, :] = vctx_ref[:, :, rows:rows + blk, :]
    uext_ref[:, 0:MAX_POOL, :] = uext_ref[:, rows:rows + MAX_POOL, :]


def _t5_bucket_table(n_buckets):
    qi = jnp.arange(WINDOW)[:, None]
    kj = jnp.arange(2 * WINDOW)[None, :]
    n = jnp.maximum(qi + WINDOW - kj, 0)
    max_exact = n_buckets // 2
    nf = jnp.maximum(n, 1).astype(F32)
    large = max_exact + (jnp.log(nf / max_exact) / math.log(MAX_DISTANCE / max_exact)
                         * (n_buckets - max_exact)).astype(jnp.int32)
    large = jnp.minimum(large, n_buckets - 1)
    return jnp.where(n < max_exact, n, large).astype(jnp.int32)


def _mixer(x2, seq, gain, w_in, q_norm, k_norm, rel_bias, sinks, pool_w, pool_scale, w_out):
    n_tok, d = x2.shape
    head_dim = q_norm.shape[0]
    n_buckets, n_heads = rel_bias.shape
    n_groups, pool_g, _ = pool_w.shape
    attn_w = n_heads * head_dim
    pool_width = n_groups * pool_g
    kv_w = (w_in.shape[1] - attn_w - pool_width) // 2
    n_kv = kv_w // head_dim
    group = n_heads // n_kv
    rows = TILE_ROWS
    n_seq = MIX_SEQS
    batch = n_tok // seq
    assert n_groups == len(POOL_WINDOWS) and pool_g == V7X_LANES
    assert 2 * head_dim == V7X_LANES and kv_w == V7X_LANES and group % 2 == 0
    assert seq % rows == 0 and rows % WINDOW == 0 and w_out.shape[0] == attn_w + pool_width
    assert batch % n_seq == 0

    assert head_dim & (head_dim - 1) == 0
    lane_head = jnp.arange(V7X_LANES) // head_dim
    mean_bd = jnp.where(lane_head[:, None] == lane_head[None, :], 1.0 / head_dim, 0.0).astype(BF16)
    mean2 = jnp.concatenate([mean_bd, mean_bd], axis=0)
    q_gain = jnp.tile(q_norm, n_heads) * (LOG2E / math.sqrt(head_dim))

    row_spec = pl.BlockSpec((None, n_seq, rows, d), lambda b, j: (b, 0, j, 0))
    smem = pl.BlockSpec(memory_space=pltpu.SMEM)
    out = pl.pallas_call(
        _mixer_kernel,
        name="mixer",
        grid=(batch // n_seq, seq // rows),
        in_specs=[
            _const_spec((WINDOW, 2 * WINDOW)), smem, smem,
            row_spec, _const_spec((1, d)), _const_spec(w_in.shape),
            _const_spec((1, attn_w)), _const_spec((1, kv_w)), _const_spec(mean2.shape),
            _const_spec(pool_w.shape), _const_spec((1, pool_width)), _const_spec(w_out.shape),
        ],
        out_specs=row_spec,
        out_shape=jax.ShapeDtypeStruct((batch // n_seq, n_seq, seq, d), F32),
        scratch_shapes=[
            pltpu.VMEM((n_heads, WINDOW, 2 * WINDOW), F32),
            pltpu.VMEM((n_heads, WINDOW, 2 * WINDOW), F32),
            pltpu.VMEM((n_seq, n_kv, rows // WINDOW, group * WINDOW, V7X_LANES), BF16),
            pltpu.VMEM((n_seq, n_kv, WINDOW + rows, V7X_LANES), BF16),
            pltpu.VMEM((n_seq, n_kv, WINDOW + rows, 2 * V7X_LANES), BF16),
            pltpu.VMEM((n_seq, MAX_POOL + rows, pool_width), F32),
            pltpu.VMEM((n_seq, rows, attn_w + pool_width), BF16),
        ],
        compiler_params=pltpu.CompilerParams(
            dimension_semantics=("arbitrary", "arbitrary"), vmem_limit_bytes=VMEM_LIMIT),
    )(_t5_bucket_table(n_buckets), rel_bias * LOG2E, sinks * LOG2E,
      x2.reshape(batch // n_seq, n_seq, seq, d), gain.reshape(1, d), w_in.astype(BF16),
      q_gain.reshape(1, attn_w), jnp.tile(k_norm, n_kv).reshape(1, kv_w),
      mean2, pool_w.astype(BF16), pool_scale.reshape(1, pool_width), w_out.astype(BF16))
    return out.reshape(n_tok, d)


def kernel(x, p, ffn1_norm, ffn1_w_gu, ffn1_w_down, mix_norm, w_in, q_norm, k_norm, rel_bias,
           sinks, pool_w, pool_scale, w_out, ffn2_norm, ffn2_w_gu, ffn2_w_down, ple_norm,
           ple_w_gate, ple_b_gate, ple_w_proj, ple_post_norm):
    batch, seq, d = x.shape
    x2 = x.reshape(batch * seq, d)
    for i in range(p.shape[0]):
        x2 = _ffn(x2, ffn1_norm[i], ffn1_w_gu[i], ffn1_w_down[i])
        x2 = _mixer(x2, seq, mix_norm[i], w_in[i], q_norm[i], k_norm[i], rel_bias, sinks[i],
                    pool_w[i], pool_scale[i], w_out[i])
        x2 = _ffn(x2, ffn2_norm[i], ffn2_w_gu[i], ffn2_w_down[i],
                  ple=(p[i].reshape(batch * seq, -1), ple_norm[i], ple_w_gate[i], ple_b_gate[i],
                       ple_w_proj[i], ple_post_norm[i]))
    return x2.reshape(batch, seq, d)
```

```python
import math

import jax
import jax.numpy as jnp
from jax import lax
from jax.experimental import pallas as pl
from jax.experimental.pallas import tpu as pltpu

F32 = jnp.float32
BF16 = jnp.bfloat16

WINDOW = 128
MAX_DISTANCE = 128
POOL_WINDOWS = (2, 4, 8, 16)
MACARON_WEIGHT = 0.5
EPS = 1e-6
NEG_INF = -1e30
LOG2E = math.log2(math.e)

V7X_LANES = 128
V7X_MXU_DIM = 256
BF16_SUBLANES = 16

TILE_ROWS = 512
FFN_ROWS = 512
MIX_SEQS = 2
FFN_CHUNK = V7X_MXU_DIM
MAX_POOL = max(POOL_WINDOWS)
VMEM_LIMIT = 48 * 1024 * 1024


def _rms(x, gain):
    ms = jnp.mean(x * x, axis=-1, keepdims=True)
    return x * lax.rsqrt(ms + EPS) * gain


def _sigmoid(z):
    return 0.5 + 0.5 * jnp.tanh(0.5 * z)


def _silu(z):
    h = 0.5 * z
    return h + h * jnp.tanh(h)


def _const_spec(shape):
    zeros = (0,) * len(shape)
    return pl.BlockSpec(shape, lambda *_: zeros, pipeline_mode=pl.Buffered(1))


def _ffn_half_step(x_ref, gain_ref, wgu_ref, wd_ref, h_ref, acc_ref):
    h_ref[...] = _rms(x_ref[...], gain_ref[...]).astype(BF16)
    d_ff = wd_ref.shape[0]
    n_chunks = d_ff // FFN_CHUNK
    for c in range(n_chunks):
        cols = slice(c * FFN_CHUNK, (c + 1) * FFN_CHUNK)
        ucols = slice(d_ff + c * FFN_CHUNK, d_ff + (c + 1) * FFN_CHUNK)
        g = jnp.dot(h_ref[...], wgu_ref[:, cols], preferred_element_type=F32)
        u = jnp.dot(h_ref[...], wgu_ref[:, ucols], preferred_element_type=F32)
        act = _silu(g) * u
        down = jnp.dot(act.astype(BF16), wd_ref[cols, :], preferred_element_type=F32)
        if c == 0:
            acc_ref[...] = down
        elif c < n_chunks - 1:
            acc_ref[...] += down
    return x_ref[...] + MACARON_WEIGHT * (acc_ref[...] + down)


def _ffn_kernel(x_ref, gain_ref, wgu_ref, wd_ref, o_ref, h_ref, acc_ref):
    o_ref[...] = _ffn_half_step(x_ref, gain_ref, wgu_ref, wd_ref, h_ref, acc_ref)


def _ffn_ple_kernel(x_ref, gain_ref, wgu_ref, wd_ref, p_ref, pgain_ref, wgate_ref, bgate_ref,
                    wproj_ref, post_ref, o_ref, h_ref, acc_ref):
    x1 = _ffn_half_step(x_ref, gain_ref, wgu_ref, wd_ref, h_ref, acc_ref)
    e = jnp.dot(p_ref[...].astype(BF16), wproj_ref[...], preferred_element_type=F32)
    en = _rms(e, post_ref[...])
    hp = _rms(x1, pgain_ref[...]).astype(BF16)
    part = 2 * V7X_MXU_DIM
    for c0 in range(0, x1.shape[1], part):
        cols = slice(c0, c0 + part)
        gate = _sigmoid(
            jnp.dot(hp, wgate_ref[:, cols], preferred_element_type=F32) + bgate_ref[:, cols])
        o_ref[:, cols] = x1[:, cols] + gate * en[:, cols]


def _ffn(x2, gain, w_gu, w_down, ple=None):
    n_tok, d = x2.shape
    d_ff = w_down.shape[0]
    assert d_ff % FFN_CHUNK == 0 and n_tok % FFN_ROWS == 0 and w_gu.shape == (d, 2 * d_ff)
    row_spec = pl.BlockSpec((FFN_ROWS, d), lambda s: (s, 0))
    in_specs = [row_spec, _const_spec((1, d)), _const_spec(w_gu.shape), _const_spec(w_down.shape)]
    args = [x2, gain.reshape(1, d), w_gu.astype(BF16), w_down.astype(BF16)]
    if ple is not None:
        p2, pgain, w_gate, b_gate, w_proj, post_gain = ple
        in_specs += [pl.BlockSpec((FFN_ROWS, p2.shape[1]), lambda s: (s, 0)), _const_spec((1, d)),
                     _const_spec(w_gate.shape), _const_spec((1, d)), _const_spec(w_proj.shape),
                     _const_spec((1, d))]
        args += [p2, pgain.reshape(1, d), w_gate.astype(BF16), b_gate.reshape(1, d),
                 w_proj.astype(BF16), post_gain.reshape(1, d)]
    return pl.pallas_call(
        _ffn_kernel if ple is None else _ffn_ple_kernel,
        name="ffn" if ple is None else "ffn_ple",
        grid=(n_tok // FFN_ROWS,),
        in_specs=in_specs,
        out_specs=row_spec,
        out_shape=jax.ShapeDtypeStruct((n_tok, d), F32),
        scratch_shapes=[pltpu.VMEM((FFN_ROWS, d), BF16), pltpu.VMEM((FFN_ROWS, d), F32)],
        compiler_params=pltpu.CompilerParams(
            dimension_semantics=("arbitrary",), vmem_limit_bytes=VMEM_LIMIT),
    )(*args)


def _head_rms_cols(t, mean2, gain):
    cols = []
    for c in range(t.shape[1] // V7X_LANES):
        sl = slice(c * V7X_LANES, (c + 1) * V7X_LANES)
        tc = t[:, sl]
        sq = tc * tc
        hi = sq.astype(BF16)
        lo = (sq - hi.astype(F32)).astype(BF16)
        ms = jnp.dot(jnp.concatenate([hi, lo], axis=1), mean2, preferred_element_type=F32)
        cols.append(tc * lax.rsqrt(ms + EPS) * gain[:, sl])
    return cols


def _dup_half(t, lane, half):
    r = pltpu.roll(t, half, axis=1)
    low = lane < half
    return jnp.where(low, t, r), jnp.where(low, r, t)


def _build_score_tables(bucket_ref, relb_ref, sinks_ref, tbl_ref, fill_ref):
    n_buckets, n_heads = relb_ref.shape

    def rows8(r, carry):
        r0 = pl.multiple_of(r * 8, 8)
        bidx = bucket_ref[pl.ds(r0, 8), :]
        slot0 = lax.broadcasted_iota(jnp.int32, bidx.shape, 1) == 0
        accs = [jnp.zeros(bidx.shape, F32) for _ in range(n_heads)]
        for b in range(n_buckets):
            hit = bidx == b
            accs = [jnp.where(hit, relb_ref[b, h], accs[h]) for h in range(n_heads)]
        for h in range(n_heads):
            tbl_ref[h, pl.ds(r0, 8), :] = accs[h]
            fill_ref[h, pl.ds(r0, 8), :] = jnp.where(slot0, sinks_ref[h], NEG_INF)
        return carry

    lax.fori_loop(0, WINDOW // 8, rows8, 0)


def _mixer_kernel(bucket_ref, relb_ref, sinks_ref, x_ref, gain_ref, win_ref, qg_ref, kg_ref,
                  mean2_ref, poolw_ref, pscale_ref, wout_ref, o_ref,
                  tbl_ref, fill_ref, qs_ref, kctx_ref, vctx_ref, uext_ref, cat_ref):
    blk = WINDOW
    n_seq, rows, _ = x_ref.shape
    n_blk = rows // blk
    n_heads = tbl_ref.shape[0]
    n_kv = kctx_ref.shape[1]
    group = n_heads // n_kv
    head_dim = V7X_LANES // 2
    attn_w = n_heads * head_dim
    kv_w = n_kv * head_dim
    pool_w = cat_ref.shape[-1] - attn_w
    pool_g = pool_w // len(POOL_WINDOWS)
    j = pl.program_id(1)

    @pl.when((pl.program_id(0) == 0) & (j == 0))
    def _first_step():
        _build_score_tables(bucket_ref, relb_ref, sinks_ref, tbl_ref, fill_ref)

    @pl.when(j == 0)
    def _reset_carry():
        kctx_ref[:, :, 0:blk, :] = jnp.zeros((n_seq, n_kv, blk, V7X_LANES), BF16)
        vctx_ref[:, :, 0:blk, :] = jnp.zeros((n_seq, n_kv, blk, 2 * V7X_LANES), BF16)
        uext_ref[:, 0:MAX_POOL, :] = jnp.zeros((n_seq, MAX_POOL, pool_w), F32)

    lane = lax.broadcasted_iota(jnp.int32, (rows, V7X_LANES), 1)
    low = lane < head_dim
    ones = jnp.ones((rows, V7X_LANES), BF16)
    qi = lax.broadcasted_iota(jnp.int32, (blk, 2 * blk), 0)
    kj = lax.broadcasted_iota(jnp.int32, (blk, 2 * blk), 1)
    dist = qi + blk - kj
    band = (dist >= 0) & (dist < WINDOW)
    low_blk = lax.broadcasted_iota(jnp.int32, (blk, V7X_LANES), 1) < head_dim
    t_pos = j * rows + lax.broadcasted_iota(jnp.int32, (rows, 1), 0)
    top_row = lax.broadcasted_iota(jnp.int32, (BF16_SUBLANES, 2 * V7X_LANES), 0) == 0
    sink_row = (lax.broadcasted_iota(jnp.int32, (BF16_SUBLANES, 2 * V7X_LANES), 1)
                >= V7X_LANES).astype(F32)

    projs = []
    for s in range(n_seq):
        h = _rms(x_ref[s], gain_ref[...]).astype(BF16)
        projs.append(jnp.dot(h, win_ref[...], preferred_element_type=F32))

    for s, proj in enumerate(projs):
        q = proj[:, 0:attn_w]
        k = proj[:, attn_w:attn_w + kv_w]
        v = proj[:, attn_w + kv_w:attn_w + 2 * kv_w]
        uext_ref[s, MAX_POOL:MAX_POOL + rows, :] = proj[:, attn_w + 2 * kv_w:]
        for c, qc in enumerate(_head_rms_cols(q, mean2_ref[...], qg_ref[...])):
            q_lo = jnp.where(low, qc, 0.0).astype(BF16)
            q_hi = jnp.where(low, 0.0, qc).astype(BF16)
            kvh, slot = divmod(2 * c, group)
            for b in range(n_blk):
                qs_ref[s, kvh, b, slot * blk:(slot + 1) * blk, :] = q_lo[b * blk:(b + 1) * blk]
                qs_ref[s, kvh, b, (slot + 1) * blk:(slot + 2) * blk, :] = q_hi[b * blk:(b + 1) * blk]
        (kn,) = _head_rms_cols(k, mean2_ref[...], kg_ref[...])
        kv_pairs = zip(_dup_half(kn, lane, head_dim), _dup_half(v, lane, head_dim))
        for kvh, (kd, vd) in enumerate(kv_pairs):
            kctx_ref[s, kvh, blk:blk + rows, :] = kd.astype(BF16)
            vctx_ref[s, kvh, blk:blk + rows, 0:V7X_LANES] = vd.astype(BF16)
            vctx_ref[s, kvh, blk:blk + rows, V7X_LANES:] = ones

    def scores(s, b, kvh):
        kc = kctx_ref[s, kvh, b * blk:(b + 2) * blk, :]
        return lax.dot_general(qs_ref[s, kvh, b], kc, (((1,), (1,)), ((), ())),
                               preferred_element_type=F32)

    def attend(sc, s, b, kvh):
        r0 = b * blk
        valid = band if b > 0 else band & ((kj >= blk) | (j > 0))
        es = []
        for i in range(group):
            head = kvh * group + i
            si = jnp.where(valid, sc[i * blk:(i + 1) * blk] + tbl_ref[head], fill_ref[head])
            m = jnp.max(si, axis=-1, keepdims=True)
            es.append(jnp.exp2(si - m).astype(BF16))
        top = vctx_ref[s, kvh, r0:r0 + BF16_SUBLANES, :].astype(F32)
        top = jnp.where(top_row, sink_row, top).astype(BF16)
        vc = jnp.concatenate([top, vctx_ref[s, kvh, r0 + BF16_SUBLANES:r0 + 2 * blk, :]], axis=0)
        pv = jnp.dot(jnp.concatenate(es, axis=0), vc, preferred_element_type=F32)
        outs = [pv[i * blk:(i + 1) * blk, 0:V7X_LANES] / pv[i * blk:(i + 1) * blk, V7X_LANES:]
                for i in range(group)]
        for pair in range(group // 2):
            col = (kvh * group + 2 * pair) * head_dim
            both = jnp.where(low_blk, outs[2 * pair], outs[2 * pair + 1])
            cat_ref[s, r0:r0 + blk, col:col + V7X_LANES] = both.astype(BF16)

    items = [(s, b, kvh) for b in range(n_blk) for kvh in range(n_kv) for s in range(n_seq)]
    sc_next = scores(*items[0])
    for n, item in enumerate(items):
        sc = sc_next
        if n + 1 < len(items):
            sc_next = scores(*items[n + 1])
        attend(sc, *item)

    for g, w in enumerate(POOL_WINDOWS):
        cols = slice(g * pool_g, (g + 1) * pool_g)
        cnt = jnp.minimum(t_pos + 1, w).astype(F32)
        for s in range(n_seq):
            total = uext_ref[s, :, cols]
            shift = 1
            while shift < w:
                total = total + pltpu.roll(total, shift, axis=0)
                shift *= 2
            d = total[MAX_POOL:] * (1.0 / cnt) - uext_ref[s, MAX_POOL:, cols]
            y = jnp.dot(d.astype(BF16), poolw_ref[g], preferred_element_type=F32)
            cat_ref[s, :, attn_w + g * pool_g:attn_w + (g + 1) * pool_g] = (
                y * pscale_ref[:, cols]).astype(BF16)

    for s in range(n_seq):
        o_ref[s] = x_ref[s] + jnp.dot(cat_ref[s], wout_ref[...], preferred_element_type=F32)
    kctx_ref[:, :, 0:blk, :] = kctx_ref[:, :, rows:rows + blk, :]
    vctx_ref[:, :, 0:blk, :] = vctx_ref[:, :, rows:rows + blk, :]
    uext_ref[:, 0:MAX_POOL, :] = uext_ref[:, rows:rows + MAX_POOL, :]


def _t5_bucket_table(n_buckets):
    qi = jnp.arange(WINDOW)[:, None]
    kj = jnp.arange(2 * WINDOW)[None, :]
    n = jnp.maximum(qi + WINDOW - kj, 0)
    max_exact = n_buckets // 2
    nf = jnp.maximum(n, 1).astype(F32)
    large = max_exact + (jnp.log(nf / max_exact) / math.log(MAX_DISTANCE / max_exact)
                         * (n_buckets - max_exact)).astype(jnp.int32)
    large = jnp.minimum(large, n_buckets - 1)
    return jnp.where(n < max_exact, n, large).astype(jnp.int32)


def _mixer(x2, seq, gain, w_in, q_norm, k_norm, rel_bias, sinks, pool_w, pool_scale, w_out):
    n_tok, d = x2.shape
    head_dim = q_norm.shape[0]
    n_buckets, n_heads = rel_bias.shape
    n_groups, pool_g, _ = pool_w.shape
    attn_w = n_heads * head_dim
    pool_width = n_groups * pool_g
    kv_w = (w_in.shape[1] - attn_w - pool_width) // 2
    n_kv = kv_w // head_dim
    group = n_heads // n_kv
    rows = TILE_ROWS
    n_seq = MIX_SEQS
    batch = n_tok // seq
    assert n_groups == len(POOL_WINDOWS) and pool_g == V7X_LANES
    assert 2 * head_dim == V7X_LANES and kv_w == V7X_LANES and group % 2 == 0
    assert seq % rows == 0 and rows % WINDOW == 0 and w_out.shape[0] == attn_w + pool_width
    assert batch % n_seq == 0

    assert head_dim & (head_dim - 1) == 0
    lane_head = jnp.arange(V7X_LANES) // head_dim
    mean_bd = jnp.where(lane_head[:, None] == lane_head[None, :], 1.0 / head_dim, 0.0).astype(BF16)
    mean2 = jnp.concatenate([mean_bd, mean_bd], axis=0)
    q_gain = jnp.tile(q_norm, n_heads) * (LOG2E / math.sqrt(head_dim))

    row_spec = pl.BlockSpec((None, n_seq, rows, d), lambda b, j: (b, 0, j, 0))
    smem = pl.BlockSpec(memory_space=pltpu.SMEM)
    out = pl.pallas_call(
        _mixer_kernel,
        name="mixer",
        grid=(batch // n_seq, seq // rows),
        in_specs=[
            _const_spec((WINDOW, 2 * WINDOW)), smem, smem,
            row_spec, _const_spec((1, d)), _const_spec(w_in.shape),
            _const_spec((1, attn_w)), _const_spec((1, kv_w)), _const_spec(mean2.shape),
            _const_spec(pool_w.shape), _const_spec((1, pool_width)), _const_spec(w_out.shape),
        ],
        out_specs=row_spec,
        out_shape=jax.ShapeDtypeStruct((batch // n_seq, n_seq, seq, d), F32),
        scratch_shapes=[
            pltpu.VMEM((n_heads, WINDOW, 2 * WINDOW), F32),
            pltpu.VMEM((n_heads, WINDOW, 2 * WINDOW), F32),
            pltpu.VMEM((n_seq, n_kv, rows // WINDOW, group * WINDOW, V7X_LANES), BF16),
            pltpu.VMEM((n_seq, n_kv, WINDOW + rows, V7X_LANES), BF16),
            pltpu.VMEM((n_seq, n_kv, WINDOW + rows, 2 * V7X_LANES), BF16),
            pltpu.VMEM((n_seq, MAX_POOL + rows, pool_width), F32),
            pltpu.VMEM((n_seq, rows, attn_w + pool_width), BF16),
        ],
        compiler_params=pltpu.CompilerParams(
            dimension_semantics=("arbitrary", "arbitrary"), vmem_limit_bytes=VMEM_LIMIT),
    )(_t5_bucket_table(n_buckets), rel_bias * LOG2E, sinks * LOG2E,
      x2.reshape(batch // n_seq, n_seq, seq, d), gain.reshape(1, d), w_in.astype(BF16),
      q_gain.reshape(1, attn_w), jnp.tile(k_norm, n_kv).reshape(1, kv_w),
      mean2, pool_w.astype(BF16), pool_scale.reshape(1, pool_width), w_out.astype(BF16))
    return out.reshape(n_tok, d)


def kernel(x, p, ffn1_norm, ffn1_w_gu, ffn1_w_down, mix_norm, w_in, q_norm, k_norm, rel_bias,
           sinks, pool_w, pool_scale, w_out, ffn2_norm, ffn2_w_gu, ffn2_w_down, ple_norm,
           ple_w_gate, ple_b_gate, ple_w_proj, ple_post_norm):
    batch, seq, d = x.shape
    x2 = x.reshape(batch * seq, d)
    for i in range(p.shape[0]):
        x2 = _ffn(x2, ffn1_norm[i], ffn1_w_gu[i], ffn1_w_down[i])
        x2 = _mixer(x2, seq, mix_norm[i], w_in[i], q_norm[i], k_norm[i], rel_bias, sinks[i],
                    pool_w[i], pool_scale[i], w_out[i])
        x2 = _ffn(x2, ffn2_norm[i], ffn2_w_gu[i], ffn2_w_down[i],
                  ple=(p[i].reshape(batch * seq, -1), ple_norm[i], ple_w_gate[i], ple_b_gate[i],
                       ple_w_proj[i], ple_post_norm[i]))
    return x2.reshape(batch, seq, d)
```

```python
import math

import jax
import jax.numpy as jnp
from jax import lax
from jax.experimental import pallas as pl
from jax.experimental.pallas import tpu as pltpu

F32 = jnp.float32
BF16 = jnp.bfloat16

WINDOW = 128
MAX_DISTANCE = 128
POOL_WINDOWS = (2, 4, 8, 16)
MACARON_WEIGHT = 0.5
EPS = 1e-6
NEG_INF = -1e30
LOG2E = math.log2(math.e)

V7X_LANES = 128
V7X_MXU_DIM = 256
BF16_SUBLANES = 16

TILE_ROWS = 512
FFN_ROWS = 512
MIX_SEQS = 2
FFN_CHUNK = V7X_MXU_DIM
MAX_POOL = max(POOL_WINDOWS)
VMEM_LIMIT = 48 * 1024 * 1024


def _rms(x, gain):
    ms = jnp.mean(x * x, axis=-1, keepdims=True)
    return x * lax.rsqrt(ms + EPS) * gain


def _silu_of_half(h):
    return h + h * jnp.tanh(h)


def _const_spec(shape):
    zeros = (0,) * len(shape)
    return pl.BlockSpec(shape, lambda *_: zeros, pipeline_mode=pl.Buffered(1))


def _ffn_half_step(x_ref, gain_ref, wgu_ref, wd_ref, h_ref, acc_ref):
    h_ref[...] = _rms(x_ref[...], gain_ref[...]).astype(BF16)
    d_ff = wd_ref.shape[0]
    n_chunks = d_ff // FFN_CHUNK
    for c in range(n_chunks):
        cols = slice(c * FFN_CHUNK, (c + 1) * FFN_CHUNK)
        ucols = slice(d_ff + c * FFN_CHUNK, d_ff + (c + 1) * FFN_CHUNK)
        half_g = jnp.dot(h_ref[...], wgu_ref[:, cols], preferred_element_type=F32)
        u = jnp.dot(h_ref[...], wgu_ref[:, ucols], preferred_element_type=F32)
        act = _silu_of_half(half_g) * u
        down = jnp.dot(act.astype(BF16), wd_ref[cols, :], preferred_element_type=F32)
        if c == 0:
            acc_ref[...] = down
        elif c < n_chunks - 1:
            acc_ref[...] += down
    return x_ref[...] + MACARON_WEIGHT * (acc_ref[...] + down)


def _ffn_kernel(x_ref, gain_ref, wgu_ref, wd_ref, o_ref, h_ref, acc_ref):
    o_ref[...] = _ffn_half_step(x_ref, gain_ref, wgu_ref, wd_ref, h_ref, acc_ref)


def _ffn_ple_kernel(x_ref, gain_ref, wgu_ref, wd_ref, p_ref, pgain_ref, wgate_ref, bgate_ref,
                    wproj_ref, post_ref, o_ref, h_ref, acc_ref):
    x1 = _ffn_half_step(x_ref, gain_ref, wgu_ref, wd_ref, h_ref, acc_ref)
    e = jnp.dot(p_ref[...].astype(BF16), wproj_ref[...], preferred_element_type=F32)
    half_e = 0.5 * _rms(e, post_ref[...])
    base = x1 + half_e
    hp = _rms(x1, pgain_ref[...]).astype(BF16)
    part = 2 * V7X_MXU_DIM
    for c0 in range(0, x1.shape[1], part):
        cols = slice(c0, c0 + part)
        half_z = jnp.dot(hp, wgate_ref[:, cols], preferred_element_type=F32) + bgate_ref[:, cols]
        o_ref[:, cols] = base[:, cols] + half_e[:, cols] * jnp.tanh(half_z)


def _ffn(x2, gain, w_gu, w_down, ple=None):
    n_tok, d = x2.shape
    d_ff = w_down.shape[0]
    assert d_ff % FFN_CHUNK == 0 and n_tok % FFN_ROWS == 0 and w_gu.shape == (d, 2 * d_ff)
    row_spec = pl.BlockSpec((FFN_ROWS, d), lambda s: (s, 0))
    in_specs = [row_spec, _const_spec((1, d)), _const_spec(w_gu.shape), _const_spec(w_down.shape)]
    gate_half = jnp.where(jnp.arange(2 * d_ff) < d_ff, 0.5, 1.0).astype(F32)
    args = [x2, gain.reshape(1, d), (w_gu * gate_half).astype(BF16), w_down.astype(BF16)]
    if ple is not None:
        p2, pgain, w_gate, b_gate, w_proj, post_gain = ple
        in_specs += [pl.BlockSpec((FFN_ROWS, p2.shape[1]), lambda s: (s, 0)), _const_spec((1, d)),
                     _const_spec(w_gate.shape), _const_spec((1, d)), _const_spec(w_proj.shape),
                     _const_spec((1, d))]
        args += [p2, pgain.reshape(1, d), (0.5 * w_gate).astype(BF16), (0.5 * b_gate).reshape(1, d),
                 w_proj.astype(BF16), post_gain.reshape(1, d)]
    return pl.pallas_call(
        _ffn_kernel if ple is None else _ffn_ple_kernel,
        name="ffn" if ple is None else "ffn_ple",
        grid=(n_tok // FFN_ROWS,),
        in_specs=in_specs,
        out_specs=row_spec,
        out_shape=jax.ShapeDtypeStruct((n_tok, d), F32),
        scratch_shapes=[pltpu.VMEM((FFN_ROWS, d), BF16), pltpu.VMEM((FFN_ROWS, d), F32)],
        compiler_params=pltpu.CompilerParams(
            dimension_semantics=("arbitrary",), vmem_limit_bytes=VMEM_LIMIT),
    )(*args)


def _head_rms_cols(t, mean2, gain):
    cols = []
    for c in range(t.shape[1] // V7X_LANES):
        sl = slice(c * V7X_LANES, (c + 1) * V7X_LANES)
        tc = t[:, sl]
        sq = tc * tc
        hi = sq.astype(BF16)
        lo = (sq - hi.astype(F32)).astype(BF16)
        ms = jnp.dot(jnp.concatenate([hi, lo], axis=1), mean2, preferred_element_type=F32)
        cols.append(tc * lax.rsqrt(ms + EPS) * gain[:, sl])
    return cols


def _dup_half(t, lane, half):
    r = pltpu.roll(t, half, axis=1)
    low = lane < half
    return jnp.where(low, t, r), jnp.where(low, r, t)


def _build_score_tables(bucket_ref, relb_ref, sinks_ref, tbl_ref, fill_ref):
    n_buckets, n_heads = relb_ref.shape

    def rows8(r, carry):
        r0 = pl.multiple_of(r * 8, 8)
        bidx = bucket_ref[pl.ds(r0, 8), :]
        slot0 = lax.broadcasted_iota(jnp.int32, bidx.shape, 1) == 0
        accs = [jnp.zeros(bidx.shape, F32) for _ in range(n_heads)]
        for b in range(n_buckets):
            hit = bidx == b
            accs = [jnp.where(hit, relb_ref[b, h], accs[h]) for h in range(n_heads)]
        for h in range(n_heads):
            tbl_ref[h, pl.ds(r0, 8), :] = accs[h]
            fill_ref[h, pl.ds(r0, 8), :] = jnp.where(slot0, sinks_ref[h], NEG_INF)
        return carry

    lax.fori_loop(0, WINDOW // 8, rows8, 0)


def _mixer_kernel(bucket_ref, relb_ref, sinks_ref, x_ref, gain_ref, win_ref, qg_ref, kg_ref,
                  mean2_ref, poolw_ref, pscale_ref, wout_ref, o_ref,
                  tbl_ref, fill_ref, qs_ref, kctx_ref, vctx_ref, uext_ref, cat_ref):
    blk = WINDOW
    n_seq, rows, _ = x_ref.shape
    n_blk = rows // blk
    n_heads = tbl_ref.shape[0]
    n_kv = kctx_ref.shape[1]
    group = n_heads // n_kv
    head_dim = V7X_LANES // 2
    attn_w = n_heads * head_dim
    kv_w = n_kv * head_dim
    pool_w = cat_ref.shape[-1] - attn_w
    pool_g = pool_w // len(POOL_WINDOWS)
    j = pl.program_id(1)

    @pl.when((pl.program_id(0) == 0) & (j == 0))
    def _first_step():
        _build_score_tables(bucket_ref, relb_ref, sinks_ref, tbl_ref, fill_ref)

    @pl.when(j == 0)
    def _reset_carry():
        kctx_ref[:, :, 0:blk, :] = jnp.zeros((n_seq, n_kv, blk, V7X_LANES), BF16)
        vctx_ref[:, :, 0:blk, :] = jnp.zeros((n_seq, n_kv, blk, 2 * V7X_LANES), BF16)
        uext_ref[:, 0:MAX_POOL, :] = jnp.zeros((n_seq, MAX_POOL, pool_w), F32)

    lane = lax.broadcasted_iota(jnp.int32, (rows, V7X_LANES), 1)
    low = lane < head_dim
    ones = jnp.ones((rows, V7X_LANES), BF16)
    qi = lax.broadcasted_iota(jnp.int32, (blk, 2 * blk), 0)
    kj = lax.broadcasted_iota(jnp.int32, (blk, 2 * blk), 1)
    dist = qi + blk - kj
    band = (dist >= 0) & (dist < WINDOW)
    low_blk = lax.broadcasted_iota(jnp.int32, (blk, V7X_LANES), 1) < head_dim
    t_pos = j * rows + lax.broadcasted_iota(jnp.int32, (rows, 1), 0)
    top_row = lax.broadcasted_iota(jnp.int32, (BF16_SUBLANES, 2 * V7X_LANES), 0) == 0
    sink_row = (lax.broadcasted_iota(jnp.int32, (BF16_SUBLANES, 2 * V7X_LANES), 1)
                >= V7X_LANES).astype(F32)

    projs = []
    for s in range(n_seq):
        h = _rms(x_ref[s], gain_ref[...]).astype(BF16)
        projs.append(jnp.dot(h, win_ref[...], preferred_element_type=F32))

    for s, proj in enumerate(projs):
        q = proj[:, 0:attn_w]
        k = proj[:, attn_w:attn_w + kv_w]
        v = proj[:, attn_w + kv_w:attn_w + 2 * kv_w]
        uext_ref[s, MAX_POOL:MAX_POOL + rows, :] = proj[:, attn_w + 2 * kv_w:]
        for c, qc in enumerate(_head_rms_cols(q, mean2_ref[...], qg_ref[...])):
            q_lo = jnp.where(low, qc, 0.0).astype(BF16)
            q_hi = jnp.where(low, 0.0, qc).astype(BF16)
            kvh, slot = divmod(2 * c, group)
            for b in range(n_blk):
                qs_ref[s, kvh, b, slot * blk:(slot + 1) * blk, :] = q_lo[b * blk:(b + 1) * blk]
                qs_ref[s, kvh, b, (slot + 1) * blk:(slot + 2) * blk, :] = q_hi[b * blk:(b + 1) * blk]
        (kn,) = _head_rms_cols(k, mean2_ref[...], kg_ref[...])
        kv_pairs = zip(_dup_half(kn, lane, head_dim), _dup_half(v, lane, head_dim))
        for kvh, (kd, vd) in enumerate(kv_pairs):
            kctx_ref[s, kvh, blk:blk + rows, :] = kd.astype(BF16)
            vctx_ref[s, kvh, blk:blk + rows, 0:V7X_LANES] = vd.astype(BF16)
            vctx_ref[s, kvh, blk:blk + rows, V7X_LANES:] = ones

    def scores(s, b, kvh):
        kc = kctx_ref[s, kvh, b * blk:(b + 2) * blk, :]
        return lax.dot_general(qs_ref[s, kvh, b], kc, (((1,), (1,)), ((), ())),
                               preferred_element_type=F32)

    def attend(sc, s, b, kvh):
        r0 = b * blk
        valid = band if b > 0 else band & ((kj >= blk) | (j > 0))
        es = []
        for i in range(group):
            head = kvh * group + i
            si = jnp.where(valid, sc[i * blk:(i + 1) * blk] + tbl_ref[head], fill_ref[head])
            m = jnp.max(si, axis=-1, keepdims=True)
            es.append(jnp.exp2(si - m).astype(BF16))
        top = vctx_ref[s, kvh, r0:r0 + BF16_SUBLANES, :].astype(F32)
        top = jnp.where(top_row, sink_row, top).astype(BF16)
        vc = jnp.concatenate([top, vctx_ref[s, kvh, r0 + BF16_SUBLANES:r0 + 2 * blk, :]], axis=0)
        pv = jnp.dot(jnp.concatenate(es, axis=0), vc, preferred_element_type=F32)
        outs = [pv[i * blk:(i + 1) * blk, 0:V7X_LANES] / pv[i * blk:(i + 1) * blk, V7X_LANES:]
                for i in range(group)]
        for pair in range(group // 2):
            col = (kvh * group + 2 * pair) * head_dim
            both = jnp.where(low_blk, outs[2 * pair], outs[2 * pair + 1])
            cat_ref[s, r0:r0 + blk, col:col + V7X_LANES] = both.astype(BF16)

    items = [(s, b, kvh) for b in range(n_blk) for kvh in range(n_kv) for s in range(n_seq)]
    sc_next = scores(*items[0])
    for n, item in enumerate(items):
        sc = sc_next
        if n + 1 < len(items):
            sc_next = scores(*items[n + 1])
        attend(sc, *item)

    for g, w in enumerate(POOL_WINDOWS):
        cols = slice(g * pool_g, (g + 1) * pool_g)
        cnt = jnp.minimum(t_pos + 1, w).astype(F32)
        for s in range(n_seq):
            total = uext_ref[s, :, cols]
            shift = 1
            while shift < w:
                total = total + pltpu.roll(total, shift, axis=0)
                shift *= 2
            d = total[MAX_POOL:] * (1.0 / cnt) - uext_ref[s, MAX_POOL:, cols]
            y = jnp.dot(d.astype(BF16), poolw_ref[g], preferred_element_type=F32)
            cat_ref[s, :, attn_w + g * pool_g:attn_w + (g + 1) * pool_g] = (
                y * pscale_ref[:, cols]).astype(BF16)

    for s in range(n_seq):
        o_ref[s] = x_ref[s] + jnp.dot(cat_ref[s], wout_ref[...], preferred_element_type=F32)
    kctx_ref[:, :, 0:blk, :] = kctx_ref[:, :, rows:rows + blk, :]
    vctx_ref[:, :, 0:blk, :] = vctx_ref[:, :, rows:rows + blk, :]
    uext_ref[:, 0:MAX_POOL, :] = uext_ref[:, rows:rows + MAX_POOL, :]


def _t5_bucket_table(n_buckets):
    qi = jnp.arange(WINDOW)[:, None]
    kj = jnp.arange(2 * WINDOW)[None, :]
    n = jnp.maximum(qi + WINDOW - kj, 0)
    max_exact = n_buckets // 2
    nf = jnp.maximum(n, 1).astype(F32)
    large = max_exact + (jnp.log(nf / max_exact) / math.log(MAX_DISTANCE / max_exact)
                         * (n_buckets - max_exact)).astype(jnp.int32)
    large = jnp.minimum(large, n_buckets - 1)
    return jnp.where(n < max_exact, n, large).astype(jnp.int32)


def _mixer(x2, seq, gain, w_in, q_norm, k_norm, rel_bias, sinks, pool_w, pool_scale, w_out):
    n_tok, d = x2.shape
    head_dim = q_norm.shape[0]
    n_buckets, n_heads = rel_bias.shape
    n_groups, pool_g, _ = pool_w.shape
    attn_w = n_heads * head_dim
    pool_width = n_groups * pool_g
    kv_w = (w_in.shape[1] - attn_w - pool_width) // 2
    n_kv = kv_w // head_dim
    group = n_heads // n_kv
    rows = TILE_ROWS
    n_seq = MIX_SEQS
    batch = n_tok // seq
    assert n_groups == len(POOL_WINDOWS) and pool_g == V7X_LANES
    assert 2 * head_dim == V7X_LANES and kv_w == V7X_LANES and group % 2 == 0
    assert seq % rows == 0 and rows % WINDOW == 0 and w_out.shape[0] == attn_w + pool_width
    assert batch % n_seq == 0

    assert head_dim & (head_dim - 1) == 0
    lane_head = jnp.arange(V7X_LANES) // head_dim
    mean_bd = jnp.where(lane_head[:, None] == lane_head[None, :], 1.0 / head_dim, 0.0).astype(BF16)
    mean2 = jnp.concatenate([mean_bd, mean_bd], axis=0)
    q_gain = jnp.tile(q_norm, n_heads) * (LOG2E / math.sqrt(head_dim))

    row_spec = pl.BlockSpec((None, n_seq, rows, d), lambda b, j: (b, 0, j, 0))
    smem = pl.BlockSpec(memory_space=pltpu.SMEM)
    out = pl.pallas_call(
        _mixer_kernel,
        name="mixer",
        grid=(batch // n_seq, seq // rows),
        in_specs=[
            _const_spec((WINDOW, 2 * WINDOW)), smem, smem,
            row_spec, _const_spec((1, d)), _const_spec(w_in.shape),
            _const_spec((1, attn_w)), _const_spec((1, kv_w)), _const_spec(mean2.shape),
            _const_spec(pool_w.shape), _const_spec((1, pool_width)), _const_spec(w_out.shape),
        ],
        out_specs=row_spec,
        out_shape=jax.ShapeDtypeStruct((batch // n_seq, n_seq, seq, d), F32),
        scratch_shapes=[
            pltpu.VMEM((n_heads, WINDOW, 2 * WINDOW), F32),
            pltpu.VMEM((n_heads, WINDOW, 2 * WINDOW), F32),
            pltpu.VMEM((n_seq, n_kv, rows // WINDOW, group * WINDOW, V7X_LANES), BF16),
            pltpu.VMEM((n_seq, n_kv, WINDOW + rows, V7X_LANES), BF16),
            pltpu.VMEM((n_seq, n_kv, WINDOW + rows, 2 * V7X_LANES), BF16),
            pltpu.VMEM((n_seq, MAX_POOL + rows, pool_width), F32),
            pltpu.VMEM((n_seq, rows, attn_w + pool_width), BF16),
        ],
        compiler_params=pltpu.CompilerParams(
            dimension_semantics=("arbitrary", "arbitrary"), vmem_limit_bytes=VMEM_LIMIT),
    )(_t5_bucket_table(n_buckets), rel_bias * LOG2E, sinks * LOG2E,
      x2.reshape(batch // n_seq, n_seq, seq, d), gain.reshape(1, d), w_in.astype(BF16),
      q_gain.reshape(1, attn_w), jnp.tile(k_norm, n_kv).reshape(1, kv_w),
      mean2, pool_w.astype(BF16), pool_scale.reshape(1, pool_width), w_out.astype(BF16))
    return out.reshape(n_tok, d)


def kernel(x, p, ffn1_norm, ffn1_w_gu, ffn1_w_down, mix_norm, w_in, q_norm, k_norm, rel_bias,
           sinks, pool_w, pool_scale, w_out, ffn2_norm, ffn2_w_gu, ffn2_w_down, ple_norm,
           ple_w_gate, ple_b_gate, ple_w_proj, ple_post_norm):
    batch, seq, d = x.shape
    x2 = x.reshape(batch * seq, d)
    for i in range(p.shape[0]):
        x2 = _ffn(x2, ffn1_norm[i], ffn1_w_gu[i], ffn1_w_down[i])
        x2 = _mixer(x2, seq, mix_norm[i], w_in[i], q_norm[i], k_norm[i], rel_bias, sinks[i],
                    pool_w[i], pool_scale[i], w_out[i])
        x2 = _ffn(x2, ffn2_norm[i], ffn2_w_gu[i], ffn2_w_down[i],
                  ple=(p[i].reshape(batch * seq, -1), ple_norm[i], ple_w_gate[i], ple_b_gate[i],
                       ple_w_proj[i], ple_post_norm[i]))
    return x2.reshape(batch, seq, d)
```

```python
import math

import jax
import jax.numpy as jnp
from jax import lax
from jax.experimental import pallas as pl
from jax.experimental.pallas import tpu as pltpu

F32 = jnp.float32
BF16 = jnp.bfloat16

WINDOW = 128
MAX_DISTANCE = 128
POOL_WINDOWS = (2, 4, 8, 16)
MACARON_WEIGHT = 0.5
EPS = 1e-6
NEG_INF = -1e30
LOG2E = math.log2(math.e)

V7X_LANES = 128
V7X_MXU_DIM = 256
BF16_SUBLANES = 16

TILE_ROWS = 512
FFN_ROWS = 512
MIX_SEQS = 2
FFN_CHUNK = V7X_MXU_DIM
MAX_POOL = max(POOL_WINDOWS)
VMEM_LIMIT = 48 * 1024 * 1024


def _rms(x, gain):
    ms = jnp.mean(x * x, axis=-1, keepdims=True)
    return x * lax.rsqrt(ms + EPS) * gain


def _silu_of_half(h):
    return h + h * jnp.tanh(h)


def _const_spec(shape):
    zeros = (0,) * len(shape)
    return pl.BlockSpec(shape, lambda *_: zeros, pipeline_mode=pl.Buffered(1))


def _ffn_half_step(x_ref, gain_ref, wgu_ref, wd_ref, h_ref, acc_ref):
    h_ref[...] = _rms(x_ref[...], gain_ref[...]).astype(BF16)
    d_ff = wd_ref.shape[0]
    n_chunks = d_ff // FFN_CHUNK
    for c in range(n_chunks):
        cols = slice(c * FFN_CHUNK, (c + 1) * FFN_CHUNK)
        ucols = slice(d_ff + c * FFN_CHUNK, d_ff + (c + 1) * FFN_CHUNK)
        half_g = jnp.dot(h_ref[...], wgu_ref[:, cols], preferred_element_type=F32)
        u = jnp.dot(h_ref[...], wgu_ref[:, ucols], preferred_element_type=F32)
        act = _silu_of_half(half_g) * u
        down = jnp.dot(act.astype(BF16), wd_ref[cols, :], preferred_element_type=F32)
        if c == 0:
            acc_ref[...] = down
        elif c < n_chunks - 1:
            acc_ref[...] += down
    return x_ref[...] + MACARON_WEIGHT * (acc_ref[...] + down)


def _ffn_kernel(x_ref, gain_ref, wgu_ref, wd_ref, o_ref, h_ref, acc_ref):
    o_ref[...] = _ffn_half_step(x_ref, gain_ref, wgu_ref, wd_ref, h_ref, acc_ref)


def _ffn_ple_kernel(x_ref, gain_ref, wgu_ref, wd_ref, p_ref, pgain_ref, wgate_ref, bgate_ref,
                    wproj_ref, post_ref, o_ref, h_ref, acc_ref):
    x1 = _ffn_half_step(x_ref, gain_ref, wgu_ref, wd_ref, h_ref, acc_ref)
    e = jnp.dot(p_ref[...].astype(BF16), wproj_ref[...], preferred_element_type=F32)
    half_e = 0.5 * _rms(e, post_ref[...])
    base = x1 + half_e
    hp = _rms(x1, pgain_ref[...]).astype(BF16)
    part = 2 * V7X_MXU_DIM
    for c0 in range(0, x1.shape[1], part):
        cols = slice(c0, c0 + part)
        half_z = jnp.dot(hp, wgate_ref[:, cols], preferred_element_type=F32) + bgate_ref[:, cols]
        o_ref[:, cols] = base[:, cols] + half_e[:, cols] * jnp.tanh(half_z)


def _ffn(x2, gain, w_gu, w_down, ple=None):
    n_tok, d = x2.shape
    d_ff = w_down.shape[0]
    assert d_ff % FFN_CHUNK == 0 and n_tok % FFN_ROWS == 0 and w_gu.shape == (d, 2 * d_ff)
    row_spec = pl.BlockSpec((FFN_ROWS, d), lambda s: (s, 0))
    in_specs = [row_spec, _const_spec((1, d)), _const_spec(w_gu.shape), _const_spec(w_down.shape)]
    gate_half = jnp.where(jnp.arange(2 * d_ff) < d_ff, 0.5, 1.0).astype(F32)
    args = [x2, gain.reshape(1, d), (w_gu * gate_half).astype(BF16), w_down.astype(BF16)]
    if ple is not None:
        p2, pgain, w_gate, b_gate, w_proj, post_gain = ple
        in_specs += [pl.BlockSpec((FFN_ROWS, p2.shape[1]), lambda s: (s, 0)), _const_spec((1, d)),
                     _const_spec(w_gate.shape), _const_spec((1, d)), _const_spec(w_proj.shape),
                     _const_spec((1, d))]
        args += [p2, pgain.reshape(1, d), (0.5 * w_gate).astype(BF16), (0.5 * b_gate).reshape(1, d),
                 w_proj.astype(BF16), post_gain.reshape(1, d)]
    return pl.pallas_call(
        _ffn_kernel if ple is None else _ffn_ple_kernel,
        name="ffn" if ple is None else "ffn_ple",
        grid=(n_tok // FFN_ROWS,),
        in_specs=in_specs,
        out_specs=row_spec,
        out_shape=jax.ShapeDtypeStruct((n_tok, d), F32),
        scratch_shapes=[pltpu.VMEM((FFN_ROWS, d), BF16), pltpu.VMEM((FFN_ROWS, d), F32)],
        compiler_params=pltpu.CompilerParams(
            dimension_semantics=("arbitrary",), vmem_limit_bytes=VMEM_LIMIT),
    )(*args)


def _head_rms_cols(t, low, gain):
    heads_per_col = 2
    cols = []
    for c in range(t.shape[1] // V7X_LANES):
        sl = slice(c * V7X_LANES, (c + 1) * V7X_LANES)
        tc = t[:, sl]
        sq = tc * tc
        ss_low = jnp.sum(jnp.where(low, sq, 0.0), axis=-1, keepdims=True)
        ss_high = jnp.sum(jnp.where(low, 0.0, sq), axis=-1, keepdims=True)
        ms = jnp.where(low, ss_low, ss_high) * (heads_per_col / V7X_LANES)
        cols.append(tc * lax.rsqrt(ms + EPS) * gain[:, sl])
    return cols


def _dup_half(t, lane, half):
    r = pltpu.roll(t, half, axis=1)
    low = lane < half
    return jnp.where(low, t, r), jnp.where(low, r, t)


def _build_score_tables(bucket_ref, relb_ref, sinks_ref, tbl_ref, fill_ref):
    n_buckets, n_heads = relb_ref.shape

    def rows8(r, carry):
        r0 = pl.multiple_of(r * 8, 8)
        bidx = bucket_ref[pl.ds(r0, 8), :]
        slot0 = lax.broadcasted_iota(jnp.int32, bidx.shape, 1) == 0
        accs = [jnp.zeros(bidx.shape, F32) for _ in range(n_heads)]
        for b in range(n_buckets):
            hit = bidx == b
            accs = [jnp.where(hit, relb_ref[b, h], accs[h]) for h in range(n_heads)]
        for h in range(n_heads):
            tbl_ref[h, pl.ds(r0, 8), :] = accs[h]
            fill_ref[h, pl.ds(r0, 8), :] = jnp.where(slot0, sinks_ref[h], NEG_INF)
        return carry

    lax.fori_loop(0, WINDOW // 8, rows8, 0)


def _mixer_kernel(bucket_ref, relb_ref, sinks_ref, x_ref, gain_ref, win_ref, qg_ref, kg_ref,
                  poolw_ref, pscale_ref, wout_ref, o_ref,
                  tbl_ref, fill_ref, qs_ref, kctx_ref, vctx_ref, uext_ref, cat_ref):
    blk = WINDOW
    n_seq, rows, _ = x_ref.shape
    n_blk = rows // blk
    n_heads = tbl_ref.shape[0]
    n_kv = kctx_ref.shape[1]
    group = n_heads // n_kv
    head_dim = V7X_LANES // 2
    attn_w = n_heads * head_dim
    kv_w = n_kv * head_dim
    pool_w = cat_ref.shape[-1] - attn_w
    pool_g = pool_w // len(POOL_WINDOWS)
    j = pl.program_id(1)

    @pl.when((pl.program_id(0) == 0) & (j == 0))
    def _first_step():
        _build_score_tables(bucket_ref, relb_ref, sinks_ref, tbl_ref, fill_ref)

    @pl.when(j == 0)
    def _reset_carry():
        kctx_ref[:, :, 0:blk, :] = jnp.zeros((n_seq, n_kv, blk, V7X_LANES), BF16)
        vctx_ref[:, :, 0:blk, :] = jnp.zeros((n_seq, n_kv, blk, 2 * V7X_LANES), BF16)
        uext_ref[:, 0:MAX_POOL, :] = jnp.zeros((n_seq, MAX_POOL, pool_w), F32)

    lane = lax.broadcasted_iota(jnp.int32, (rows, V7X_LANES), 1)
    low = lane < head_dim
    ones = jnp.ones((rows, V7X_LANES), BF16)
    qi = lax.broadcasted_iota(jnp.int32, (blk, 2 * blk), 0)
    kj = lax.broadcasted_iota(jnp.int32, (blk, 2 * blk), 1)
    dist = qi + blk - kj
    band = (dist >= 0) & (dist < WINDOW)
    low_blk = lax.broadcasted_iota(jnp.int32, (blk, V7X_LANES), 1) < head_dim
    t_pos = j * rows + lax.broadcasted_iota(jnp.int32, (rows, 1), 0)
    top_row = lax.broadcasted_iota(jnp.int32, (BF16_SUBLANES, 2 * V7X_LANES), 0) == 0
    sink_row = (lax.broadcasted_iota(jnp.int32, (BF16_SUBLANES, 2 * V7X_LANES), 1)
                >= V7X_LANES).astype(F32)

    projs = []
    for s in range(n_seq):
        h = _rms(x_ref[s], gain_ref[...]).astype(BF16)
        projs.append(jnp.dot(h, win_ref[...], preferred_element_type=F32))

    for s, proj in enumerate(projs):
        q = proj[:, 0:attn_w]
        k = proj[:, attn_w:attn_w + kv_w]
        v = proj[:, attn_w + kv_w:attn_w + 2 * kv_w]
        uext_ref[s, MAX_POOL:MAX_POOL + rows, :] = proj[:, attn_w + 2 * kv_w:]
        for c, qc in enumerate(_head_rms_cols(q, low, qg_ref[...])):
            q_lo = jnp.where(low, qc, 0.0).astype(BF16)
            q_hi = jnp.where(low, 0.0, qc).astype(BF16)
            kvh, slot = divmod(2 * c, group)
            for b in range(n_blk):
                qs_ref[s, kvh, b, slot * blk:(slot + 1) * blk, :] = q_lo[b * blk:(b + 1) * blk]
                qs_ref[s, kvh, b, (slot + 1) * blk:(slot + 2) * blk, :] = q_hi[b * blk:(b + 1) * blk]
        (kn,) = _head_rms_cols(k, low, kg_ref[...])
        kv_pairs = zip(_dup_half(kn, lane, head_dim), _dup_half(v, lane, head_dim))
        for kvh, (kd, vd) in enumerate(kv_pairs):
            kctx_ref[s, kvh, blk:blk + rows, :] = kd.astype(BF16)
            vctx_ref[s, kvh, blk:blk + rows, 0:V7X_LANES] = vd.astype(BF16)
            vctx_ref[s, kvh, blk:blk + rows, V7X_LANES:] = ones

    def scores(s, b, kvh):
        kc = kctx_ref[s, kvh, b * blk:(b + 2) * blk, :]
        return lax.dot_general(qs_ref[s, kvh, b], kc, (((1,), (1,)), ((), ())),
                               preferred_element_type=F32)

    def attend(sc, s, b, kvh):
        r0 = b * blk
        valid = band if b > 0 else band & ((kj >= blk) | (j > 0))
        es = []
        for i in range(group):
            head = kvh * group + i
            si = jnp.where(valid, sc[i * blk:(i + 1) * blk] + tbl_ref[head], fill_ref[head])
            m = jnp.max(si, axis=-1, keepdims=True)
            es.append(jnp.exp2(si - m).astype(BF16))
        top = vctx_ref[s, kvh, r0:r0 + BF16_SUBLANES, :].astype(F32)
        top = jnp.where(top_row, sink_row, top).astype(BF16)
        vc = jnp.concatenate([top, vctx_ref[s, kvh, r0 + BF16_SUBLANES:r0 + 2 * blk, :]], axis=0)
        pv = jnp.dot(jnp.concatenate(es, axis=0), vc, preferred_element_type=F32)
        outs = [pv[i * blk:(i + 1) * blk, 0:V7X_LANES] / pv[i * blk:(i + 1) * blk, V7X_LANES:]
                for i in range(group)]
        for pair in range(group // 2):
            col = (kvh * group + 2 * pair) * head_dim
            both = jnp.where(low_blk, outs[2 * pair], outs[2 * pair + 1])
            cat_ref[s, r0:r0 + blk, col:col + V7X_LANES] = both.astype(BF16)

    items = [(s, b, kvh) for b in range(n_blk) for kvh in range(n_kv) for s in range(n_seq)]
    sc_next = scores(*items[0])
    for n, item in enumerate(items):
        sc = sc_next
        if n + 1 < len(items):
            sc_next = scores(*items[n + 1])
        attend(sc, *item)

    for g, w in enumerate(POOL_WINDOWS):
        cols = slice(g * pool_g, (g + 1) * pool_g)
        cnt = jnp.minimum(t_pos + 1, w).astype(F32)
        for s in range(n_seq):
            total = uext_ref[s, :, cols]
            shift = 1
            while shift < w:
                total = total + pltpu.roll(total, shift, axis=0)
                shift *= 2
            d = total[MAX_POOL:] * (1.0 / cnt) - uext_ref[s, MAX_POOL:, cols]
            y = jnp.dot(d.astype(BF16), poolw_ref[g], preferred_element_type=F32)
            cat_ref[s, :, attn_w + g * pool_g:attn_w + (g + 1) * pool_g] = (
                y * pscale_ref[:, cols]).astype(BF16)

    for s in range(n_seq):
        o_ref[s] = x_ref[s] + jnp.dot(cat_ref[s], wout_ref[...], preferred_element_type=F32)
    kctx_ref[:, :, 0:blk, :] = kctx_ref[:, :, rows:rows + blk, :]
    vctx_ref[:, :, 0:blk, :] = vctx_ref[:, :, rows:rows + blk, :]
    uext_ref[:, 0:MAX_POOL, :] = uext_ref[:, rows:rows + MAX_POOL, :]


def _t5_bucket_table(n_buckets):
    qi = jnp.arange(WINDOW)[:, None]
    kj = jnp.arange(2 * WINDOW)[None, :]
    n = jnp.maximum(qi + WINDOW - kj, 0)
    max_exact = n_buckets // 2
    nf = jnp.maximum(n, 1).astype(F32)
    large = max_exact + (jnp.log(nf / max_exact) / math.log(MAX_DISTANCE / max_exact)
                         * (n_buckets - max_exact)).astype(jnp.int32)
    large = jnp.minimum(large, n_buckets - 1)
    return jnp.where(n < max_exact, n, large).astype(jnp.int32)


def _mixer(x2, seq, gain, w_in, q_norm, k_norm, rel_bias, sinks, pool_w, pool_scale, w_out):
    n_tok, d = x2.shape
    head_dim = q_norm.shape[0]
    n_buckets, n_heads = rel_bias.shape
    n_groups, pool_g, _ = pool_w.shape
    attn_w = n_heads * head_dim
    pool_width = n_groups * pool_g
    kv_w = (w_in.shape[1] - attn_w - pool_width) // 2
    n_kv = kv_w // head_dim
    group = n_heads // n_kv
    rows = TILE_ROWS
    n_seq = MIX_SEQS
    batch = n_tok // seq
    assert n_groups == len(POOL_WINDOWS) and pool_g == V7X_LANES
    assert 2 * head_dim == V7X_LANES and kv_w == V7X_LANES and group % 2 == 0
    assert seq % rows == 0 and rows % WINDOW == 0 and w_out.shape[0] == attn_w + pool_width
    assert batch % n_seq == 0

    q_gain = jnp.tile(q_norm, n_heads) * (LOG2E / math.sqrt(head_dim))

    row_spec = pl.BlockSpec((None, n_seq, rows, d), lambda b, j: (b, 0, j, 0))
    smem = pl.BlockSpec(memory_space=pltpu.SMEM)
    out = pl.pallas_call(
        _mixer_kernel,
        name="mixer",
        grid=(batch // n_seq, seq // rows),
        in_specs=[
            _const_spec((WINDOW, 2 * WINDOW)), smem, smem,
            row_spec, _const_spec((1, d)), _const_spec(w_in.shape),
            _const_spec((1, attn_w)), _const_spec((1, kv_w)),
            _const_spec(pool_w.shape), _const_spec((1, pool_width)), _const_spec(w_out.shape),
        ],
        out_specs=row_spec,
        out_shape=jax.ShapeDtypeStruct((batch // n_seq, n_seq, seq, d), F32),
        scratch_shapes=[
            pltpu.VMEM((n_heads, WINDOW, 2 * WINDOW), F32),
            pltpu.VMEM((n_heads, WINDOW, 2 * WINDOW), F32),
            pltpu.VMEM((n_seq, n_kv, rows // WINDOW, group * WINDOW, V7X_LANES), BF16),
            pltpu.VMEM((n_seq, n_kv, WINDOW + rows, V7X_LANES), BF16),
            pltpu.VMEM((n_seq, n_kv, WINDOW + rows, 2 * V7X_LANES), BF16),
            pltpu.VMEM((n_seq, MAX_POOL + rows, pool_width), F32),
            pltpu.VMEM((n_seq, rows, attn_w + pool_width), BF16),
        ],
        compiler_params=pltpu.CompilerParams(
            dimension_semantics=("arbitrary", "arbitrary"), vmem_limit_bytes=VMEM_LIMIT),
    )(_t5_bucket_table(n_buckets), rel_bias * LOG2E, sinks * LOG2E,
      x2.reshape(batch // n_seq, n_seq, seq, d), gain.reshape(1, d), w_in.astype(BF16),
      q_gain.reshape(1, attn_w), jnp.tile(k_norm, n_kv).reshape(1, kv_w),
      pool_w.astype(BF16), pool_scale.reshape(1, pool_width), w_out.astype(BF16))
    return out.reshape(n_tok, d)


def kernel(x, p, ffn1_norm, ffn1_w_gu, ffn1_w_down, mix_norm, w_in, q_norm, k_norm, rel_bias,
           sinks, pool_w, pool_scale, w_out, ffn2_norm, ffn2_w_gu, ffn2_w_down, ple_norm,
           ple_w_gate, ple_b_gate, ple_w_proj, ple_post_norm):
    batch, seq, d = x.shape
    x2 = x.reshape(batch * seq, d)
    for i in range(p.shape[0]):
        x2 = _ffn(x2, ffn1_norm[i], ffn1_w_gu[i], ffn1_w_down[i])
        x2 = _mixer(x2, seq, mix_norm[i], w_in[i], q_norm[i], k_norm[i], rel_bias, sinks[i],
                    pool_w[i], pool_scale[i], w_out[i])
        x2 = _ffn(x2, ffn2_norm[i], ffn2_w_gu[i], ffn2_w_down[i],
                  ple=(p[i].reshape(batch * seq, -1), ple_norm[i], ple_w_gate[i], ple_b_gate[i],
                       ple_w_proj[i], ple_post_norm[i]))
    return x2.reshape(batch, seq, d)
```

```python
import math

import jax
import jax.numpy as jnp
from jax import lax
from jax.experimental import pallas as pl
from jax.experimental.pallas import tpu as pltpu

F32 = jnp.float32
BF16 = jnp.bfloat16

WINDOW = 128
MAX_DISTANCE = 128
POOL_WINDOWS = (2, 4, 8, 16)
MACARON_WEIGHT = 0.5
EPS = 1e-6
NEG_INF = -1e30
LOG2E = math.log2(math.e)

V7X_LANES = 128
V7X_MXU_DIM = 256
BF16_SUBLANES = 16

TILE_ROWS = 512
FFN_ROWS = 1024
MIX_SEQS = 2
FFN_CHUNK = V7X_MXU_DIM
MAX_POOL = max(POOL_WINDOWS)
VMEM_LIMIT = 56 * 1024 * 1024


def _rms(x, gain):
    ms = jnp.mean(x * x, axis=-1, keepdims=True)
    return x * lax.rsqrt(ms + EPS) * gain


def _silu_of_half(h):
    return h + h * jnp.tanh(h)


def _const_spec(shape):
    zeros = (0,) * len(shape)
    return pl.BlockSpec(shape, lambda *_: zeros, pipeline_mode=pl.Buffered(1))


def _ffn_half_step(x_ref, gain_ref, wgu_ref, wd_ref, h_ref, acc_ref):
    h_ref[...] = _rms(x_ref[...], gain_ref[...]).astype(BF16)
    d_ff = wd_ref.shape[0]
    n_chunks = d_ff // FFN_CHUNK
    for c in range(n_chunks):
        cols = slice(c * FFN_CHUNK, (c + 1) * FFN_CHUNK)
        ucols = slice(d_ff + c * FFN_CHUNK, d_ff + (c + 1) * FFN_CHUNK)
        half_g = jnp.dot(h_ref[...], wgu_ref[:, cols], preferred_element_type=F32)
        u = jnp.dot(h_ref[...], wgu_ref[:, ucols], preferred_element_type=F32)
        act = _silu_of_half(half_g) * u
        down = jnp.dot(act.astype(BF16), wd_ref[cols, :], preferred_element_type=F32)
        if c == 0:
            acc_ref[...] = down
        elif c < n_chunks - 1:
            acc_ref[...] += down
    return x_ref[...] + MACARON_WEIGHT * (acc_ref[...] + down)


def _ffn_kernel(x_ref, gain_ref, wgu_ref, wd_ref, o_ref, h_ref, acc_ref):
    o_ref[...] = _ffn_half_step(x_ref, gain_ref, wgu_ref, wd_ref, h_ref, acc_ref)


def _ffn_ple_kernel(x_ref, gain_ref, wgu_ref, wd_ref, p_ref, pgain_ref, wgate_ref, bgate_ref,
                    wproj_ref, post_ref, o_ref, h_ref, acc_ref):
    x1 = _ffn_half_step(x_ref, gain_ref, wgu_ref, wd_ref, h_ref, acc_ref)
    e = jnp.dot(p_ref[...].astype(BF16), wproj_ref[...], preferred_element_type=F32)
    half_e = 0.5 * _rms(e, post_ref[...])
    base = x1 + half_e
    hp = _rms(x1, pgain_ref[...]).astype(BF16)
    part = 2 * V7X_MXU_DIM
    for c0 in range(0, x1.shape[1], part):
        cols = slice(c0, c0 + part)
        half_z = jnp.dot(hp, wgate_ref[:, cols], preferred_element_type=F32) + bgate_ref[:, cols]
        o_ref[:, cols] = base[:, cols] + half_e[:, cols] * jnp.tanh(half_z)


def _ffn(x2, gain, w_gu, w_down, ple=None):
    n_tok, d = x2.shape
    d_ff = w_down.shape[0]
    assert d_ff % FFN_CHUNK == 0 and n_tok % FFN_ROWS == 0 and w_gu.shape == (d, 2 * d_ff)
    row_spec = pl.BlockSpec((FFN_ROWS, d), lambda s: (s, 0))
    in_specs = [row_spec, _const_spec((1, d)), _const_spec(w_gu.shape), _const_spec(w_down.shape)]
    gate_half = jnp.where(jnp.arange(2 * d_ff) < d_ff, 0.5, 1.0).astype(F32)
    args = [x2, gain.reshape(1, d), (w_gu * gate_half).astype(BF16), w_down.astype(BF16)]
    if ple is not None:
        p2, pgain, w_gate, b_gate, w_proj, post_gain = ple
        in_specs += [pl.BlockSpec((FFN_ROWS, p2.shape[1]), lambda s: (s, 0)), _const_spec((1, d)),
                     _const_spec(w_gate.shape), _const_spec((1, d)), _const_spec(w_proj.shape),
                     _const_spec((1, d))]
        args += [p2, pgain.reshape(1, d), (0.5 * w_gate).astype(BF16), (0.5 * b_gate).reshape(1, d),
                 w_proj.astype(BF16), post_gain.reshape(1, d)]
    return pl.pallas_call(
        _ffn_kernel if ple is None else _ffn_ple_kernel,
        name="ffn" if ple is None else "ffn_ple",
        grid=(n_tok // FFN_ROWS,),
        in_specs=in_specs,
        out_specs=row_spec,
        out_shape=jax.ShapeDtypeStruct((n_tok, d), F32),
        scratch_shapes=[pltpu.VMEM((FFN_ROWS, d), BF16), pltpu.VMEM((FFN_ROWS, d), F32)],
        compiler_params=pltpu.CompilerParams(
            dimension_semantics=("arbitrary",), vmem_limit_bytes=VMEM_LIMIT),
    )(*args)


def _head_rms_cols(t, low, gain):
    heads_per_col = 2
    cols = []
    for c in range(t.shape[1] // V7X_LANES):
        sl = slice(c * V7X_LANES, (c + 1) * V7X_LANES)
        tc = t[:, sl]
        sq = tc * tc
        ss_low = jnp.sum(jnp.where(low, sq, 0.0), axis=-1, keepdims=True)
        ss_high = jnp.sum(jnp.where(low, 0.0, sq), axis=-1, keepdims=True)
        ms = jnp.where(low, ss_low, ss_high) * (heads_per_col / V7X_LANES)
        cols.append(tc * lax.rsqrt(ms + EPS) * gain[:, sl])
    return cols


def _dup_half(t, lane, half):
    r = pltpu.roll(t, half, axis=1)
    low = lane < half
    return jnp.where(low, t, r), jnp.where(low, r, t)


def _build_score_tables(bucket_ref, relb_ref, sinks_ref, tbl_ref, fill_ref):
    n_buckets, n_heads = relb_ref.shape

    def rows8(r, carry):
        r0 = pl.multiple_of(r * 8, 8)
        bidx = bucket_ref[pl.ds(r0, 8), :]
        slot0 = lax.broadcasted_iota(jnp.int32, bidx.shape, 1) == 0
        accs = [jnp.zeros(bidx.shape, F32) for _ in range(n_heads)]
        for b in range(n_buckets):
            hit = bidx == b
            accs = [jnp.where(hit, relb_ref[b, h], accs[h]) for h in range(n_heads)]
        for h in range(n_heads):
            tbl_ref[h, pl.ds(r0, 8), :] = accs[h]
            fill_ref[h, pl.ds(r0, 8), :] = jnp.where(slot0, sinks_ref[h], NEG_INF)
        return carry

    lax.fori_loop(0, WINDOW // 8, rows8, 0)


def _mixer_kernel(bucket_ref, relb_ref, sinks_ref, x_ref, gain_ref, win_ref, qg_ref, kg_ref,
                  poolw_ref, pscale_ref, wout_ref, o_ref,
                  tbl_ref, fill_ref, qs_ref, kctx_ref, vctx_ref, uext_ref, cat_ref):
    blk = WINDOW
    n_seq, rows, _ = x_ref.shape
    n_blk = rows // blk
    n_heads = tbl_ref.shape[0]
    n_kv = kctx_ref.shape[1]
    group = n_heads // n_kv
    head_dim = V7X_LANES // 2
    attn_w = n_heads * head_dim
    kv_w = n_kv * head_dim
    pool_w = cat_ref.shape[-1] - attn_w
    pool_g = pool_w // len(POOL_WINDOWS)
    j = pl.program_id(1)

    @pl.when((pl.program_id(0) == 0) & (j == 0))
    def _first_step():
        _build_score_tables(bucket_ref, relb_ref, sinks_ref, tbl_ref, fill_ref)

    @pl.when(j == 0)
    def _reset_carry():
        kctx_ref[:, :, 0:blk, :] = jnp.zeros((n_seq, n_kv, blk, V7X_LANES), BF16)
        vctx_ref[:, :, 0:blk, :] = jnp.zeros((n_seq, n_kv, blk, 2 * V7X_LANES), BF16)
        uext_ref[:, 0:MAX_POOL, :] = jnp.zeros((n_seq, MAX_POOL, pool_w), F32)

    lane = lax.broadcasted_iota(jnp.int32, (rows, V7X_LANES), 1)
    low = lane < head_dim
    ones = jnp.ones((rows, V7X_LANES), BF16)
    qi = lax.broadcasted_iota(jnp.int32, (blk, 2 * blk), 0)
    kj = lax.broadcasted_iota(jnp.int32, (blk, 2 * blk), 1)
    dist = qi + blk - kj
    band = (dist >= 0) & (dist < WINDOW)
    low_blk = lax.broadcasted_iota(jnp.int32, (blk, V7X_LANES), 1) < head_dim
    t_pos = j * rows + lax.broadcasted_iota(jnp.int32, (rows, 1), 0)
    top_row = lax.broadcasted_iota(jnp.int32, (BF16_SUBLANES, 2 * V7X_LANES), 0) == 0
    sink_row = (lax.broadcasted_iota(jnp.int32, (BF16_SUBLANES, 2 * V7X_LANES), 1)
                >= V7X_LANES).astype(F32)

    for s in range(n_seq):
        h = _rms(x_ref[s], gain_ref[...]).astype(BF16)
        proj = jnp.dot(h, win_ref[...], preferred_element_type=F32)
        q = proj[:, 0:attn_w]
        k = proj[:, attn_w:attn_w + kv_w]
        v = proj[:, attn_w + kv_w:attn_w + 2 * kv_w]
        uext_ref[s, MAX_POOL:MAX_POOL + rows, :] = proj[:, attn_w + 2 * kv_w:]
        for c, qc in enumerate(_head_rms_cols(q, low, qg_ref[...])):
            q_lo = jnp.where(low, qc, 0.0).astype(BF16)
            q_hi = jnp.where(low, 0.0, qc).astype(BF16)
            kvh, slot = divmod(2 * c, group)
            for b in range(n_blk):
                qs_ref[s, kvh, b, slot * blk:(slot + 1) * blk, :] = q_lo[b * blk:(b + 1) * blk]
                qs_ref[s, kvh, b, (slot + 1) * blk:(slot + 2) * blk, :] = q_hi[b * blk:(b + 1) * blk]
        (kn,) = _head_rms_cols(k, low, kg_ref[...])
        kv_pairs = zip(_dup_half(kn, lane, head_dim), _dup_half(v, lane, head_dim))
        for kvh, (kd, vd) in enumerate(kv_pairs):
            kctx_ref[s, kvh, blk:blk + rows, :] = kd.astype(BF16)
            vctx_ref[s, kvh, blk:blk + rows, 0:V7X_LANES] = vd.astype(BF16)
            vctx_ref[s, kvh, blk:blk + rows, V7X_LANES:] = ones

    def scores(s, b, kvh):
        kc = kctx_ref[s, kvh, b * blk:(b + 2) * blk, :]
        return lax.dot_general(qs_ref[s, kvh, b], kc, (((1,), (1,)), ((), ())),
                               preferred_element_type=F32)

    def attend(sc, s, b, kvh):
        r0 = b * blk
        valid = band if b > 0 else band & ((kj >= blk) | (j > 0))
        es = []
        for i in range(group):
            head = kvh * group + i
            si = jnp.where(valid, sc[i * blk:(i + 1) * blk] + tbl_ref[head], fill_ref[head])
            m = jnp.max(si, axis=-1, keepdims=True)
            es.append(jnp.exp2(si - m).astype(BF16))
        top = vctx_ref[s, kvh, r0:r0 + BF16_SUBLANES, :].astype(F32)
        top = jnp.where(top_row, sink_row, top).astype(BF16)
        vc = jnp.concatenate([top, vctx_ref[s, kvh, r0 + BF16_SUBLANES:r0 + 2 * blk, :]], axis=0)
        pv = jnp.dot(jnp.concatenate(es, axis=0), vc, preferred_element_type=F32)
        outs = [pv[i * blk:(i + 1) * blk, 0:V7X_LANES] / pv[i * blk:(i + 1) * blk, V7X_LANES:]
                for i in range(group)]
        for pair in range(group // 2):
            col = (kvh * group + 2 * pair) * head_dim
            both = jnp.where(low_blk, outs[2 * pair], outs[2 * pair + 1])
            cat_ref[s, r0:r0 + blk, col:col + V7X_LANES] = both.astype(BF16)

    items = [(s, b, kvh) for b in range(n_blk) for kvh in range(n_kv) for s in range(n_seq)]
    sc_next = scores(*items[0])
    for n, item in enumerate(items):
        sc = sc_next
        if n + 1 < len(items):
            sc_next = scores(*items[n + 1])
        attend(sc, *item)

    for g, w in enumerate(POOL_WINDOWS):
        cols = slice(g * pool_g, (g + 1) * pool_g)
        cnt = jnp.minimum(t_pos + 1, w).astype(F32)
        for s in range(n_seq):
            total = uext_ref[s, :, cols]
            shift = 1
            while shift < w:
                total = total + pltpu.roll(total, shift, axis=0)
                shift *= 2
            d = total[MAX_POOL:] * (1.0 / cnt) - uext_ref[s, MAX_POOL:, cols]
            y = jnp.dot(d.astype(BF16), poolw_ref[g], preferred_element_type=F32)
            cat_ref[s, :, attn_w + g * pool_g:attn_w + (g + 1) * pool_g] = (
                y * pscale_ref[:, cols]).astype(BF16)

    for s in range(n_seq):
        o_ref[s] = x_ref[s] + jnp.dot(cat_ref[s], wout_ref[...], preferred_element_type=F32)
    kctx_ref[:, :, 0:blk, :] = kctx_ref[:, :, rows:rows + blk, :]
    vctx_ref[:, :, 0:blk, :] = vctx_ref[:, :, rows:rows + blk, :]
    uext_ref[:, 0:MAX_POOL, :] = uext_ref[:, rows:rows + MAX_POOL, :]


def _t5_bucket_table(n_buckets):
    qi = jnp.arange(WINDOW)[:, None]
    kj = jnp.arange(2 * WINDOW)[None, :]
    n = jnp.maximum(qi + WINDOW - kj, 0)
    max_exact = n_buckets // 2
    nf = jnp.maximum(n, 1).astype(F32)
    large = max_exact + (jnp.log(nf / max_exact) / math.log(MAX_DISTANCE / max_exact)
                         * (n_buckets - max_exact)).astype(jnp.int32)
    large = jnp.minimum(large, n_buckets - 1)
    return jnp.where(n < max_exact, n, large).astype(jnp.int32)


def _mixer(x2, seq, gain, w_in, q_norm, k_norm, rel_bias, sinks, pool_w, pool_scale, w_out):
    n_tok, d = x2.shape
    head_dim = q_norm.shape[0]
    n_buckets, n_heads = rel_bias.shape
    n_groups, pool_g, _ = pool_w.shape
    attn_w = n_heads * head_dim
    pool_width = n_groups * pool_g
    kv_w = (w_in.shape[1] - attn_w - pool_width) // 2
    n_kv = kv_w // head_dim
    group = n_heads // n_kv
    rows = TILE_ROWS
    n_seq = MIX_SEQS
    batch = n_tok // seq
    assert n_groups == len(POOL_WINDOWS) and pool_g == V7X_LANES
    assert 2 * head_dim == V7X_LANES and kv_w == V7X_LANES and group % 2 == 0
    assert seq % rows == 0 and rows % WINDOW == 0 and w_out.shape[0] == attn_w + pool_width
    assert batch % n_seq == 0

    q_gain = jnp.tile(q_norm, n_heads) * (LOG2E / math.sqrt(head_dim))

    row_spec = pl.BlockSpec((None, n_seq, rows, d), lambda b, j: (b, 0, j, 0))
    smem = pl.BlockSpec(memory_space=pltpu.SMEM)
    out = pl.pallas_call(
        _mixer_kernel,
        name="mixer",
        grid=(batch // n_seq, seq // rows),
        in_specs=[
            _const_spec((WINDOW, 2 * WINDOW)), smem, smem,
            row_spec, _const_spec((1, d)), _const_spec(w_in.shape),
            _const_spec((1, attn_w)), _const_spec((1, kv_w)),
            _const_spec(pool_w.shape), _const_spec((1, pool_width)), _const_spec(w_out.shape),
        ],
        out_specs=row_spec,
        out_shape=jax.ShapeDtypeStruct((batch // n_seq, n_seq, seq, d), F32),
        scratch_shapes=[
            pltpu.VMEM((n_heads, WINDOW, 2 * WINDOW), F32),
            pltpu.VMEM((n_heads, WINDOW, 2 * WINDOW), F32),
            pltpu.VMEM((n_seq, n_kv, rows // WINDOW, group * WINDOW, V7X_LANES), BF16),
            pltpu.VMEM((n_seq, n_kv, WINDOW + rows, V7X_LANES), BF16),
            pltpu.VMEM((n_seq, n_kv, WINDOW + rows, 2 * V7X_LANES), BF16),
            pltpu.VMEM((n_seq, MAX_POOL + rows, pool_width), F32),
            pltpu.VMEM((n_seq, rows, attn_w + pool_width), BF16),
        ],
        compiler_params=pltpu.CompilerParams(
            dimension_semantics=("arbitrary", "arbitrary"), vmem_limit_bytes=VMEM_LIMIT),
    )(_t5_bucket_table(n_buckets), rel_bias * LOG2E, sinks * LOG2E,
      x2.reshape(batch // n_seq, n_seq, seq, d), gain.reshape(1, d), w_in.astype(BF16),
      q_gain.reshape(1, attn_w), jnp.tile(k_norm, n_kv).reshape(1, kv_w),
      pool_w.astype(BF16), pool_scale.reshape(1, pool_width), w_out.astype(BF16))
    return out.reshape(n_tok, d)


def kernel(x, p, ffn1_norm, ffn1_w_gu, ffn1_w_down, mix_norm, w_in, q_norm, k_norm, rel_bias,
           sinks, pool_w, pool_scale, w_out, ffn2_norm, ffn2_w_gu, ffn2_w_down, ple_norm,
           ple_w_gate, ple_b_gate, ple_w_proj, ple_post_norm):
    batch, seq, d = x.shape
    x2 = x.reshape(batch * seq, d)
    for i in range(p.shape[0]):
        x2 = _ffn(x2, ffn1_norm[i], ffn1_w_gu[i], ffn1_w_down[i])
        x2 = _mixer(x2, seq, mix_norm[i], w_in[i], q_norm[i], k_norm[i], rel_bias, sinks[i],
                    pool_w[i], pool_scale[i], w_out[i])
        x2 = _ffn(x2, ffn2_norm[i], ffn2_w_gu[i], ffn2_w_down[i],
                  ple=(p[i].reshape(batch * seq, -1), ple_norm[i], ple_w_gate[i], ple_b_gate[i],
                       ple_w_proj[i], ple_post_norm[i]))
    return x2.reshape(batch, seq, d)
```

```python
import math

import jax
import jax.numpy as jnp
from jax import lax
from jax.experimental import pallas as pl
from jax.experimental.pallas import tpu as pltpu

F32 = jnp.float32
BF16 = jnp.bfloat16

WINDOW = 128
MAX_DISTANCE = 128
POOL_WINDOWS = (2, 4, 8, 16)
MACARON_WEIGHT = 0.5
EPS = 1e-6
NEG_INF = -1e30
LOG2E = math.log2(math.e)

V7X_LANES = 128
V7X_MXU_DIM = 256
BF16_SUBLANES = 16

TILE_ROWS = 512
FFN_ROWS = 1024
MIX_SEQS = 2
FFN_CHUNK = V7X_MXU_DIM
MAX_POOL = max(POOL_WINDOWS)
VMEM_LIMIT = 56 * 1024 * 1024


def _rms(x, gain):
    ms = jnp.mean(x * x, axis=-1, keepdims=True)
    return x * lax.rsqrt(ms + EPS) * gain


def _silu_of_half(h):
    return h + h * jnp.tanh(h)


def _const_spec(shape):
    zeros = (0,) * len(shape)
    return pl.BlockSpec(shape, lambda *_: zeros, pipeline_mode=pl.Buffered(1))


def _ffn_half_step(x_ref, gain_ref, wgu_ref, wd_ref, h_ref, acc_ref):
    h_ref[...] = _rms(x_ref[...], gain_ref[...]).astype(BF16)
    d_ff = wd_ref.shape[0]
    n_chunks = d_ff // FFN_CHUNK
    for c in range(n_chunks):
        cols = slice(c * FFN_CHUNK, (c + 1) * FFN_CHUNK)
        ucols = slice(d_ff + c * FFN_CHUNK, d_ff + (c + 1) * FFN_CHUNK)
        half_g = jnp.dot(h_ref[...], wgu_ref[:, cols], preferred_element_type=F32)
        u = jnp.dot(h_ref[...], wgu_ref[:, ucols], preferred_element_type=F32)
        act = _silu_of_half(half_g) * u
        down = jnp.dot(act.astype(BF16), wd_ref[cols, :], preferred_element_type=F32)
        if c == 0:
            acc_ref[...] = down
        elif c < n_chunks - 1:
            acc_ref[...] += down
    return x_ref[...] + MACARON_WEIGHT * (acc_ref[...] + down)


def _ffn_kernel(x_ref, gain_ref, wgu_ref, wd_ref, o_ref, h_ref, acc_ref):
    o_ref[...] = _ffn_half_step(x_ref, gain_ref, wgu_ref, wd_ref, h_ref, acc_ref)


def _ffn_ple_kernel(x_ref, gain_ref, wgu_ref, wd_ref, p_ref, pgain_ref, wgate_ref, bgate_ref,
                    wproj_ref, post_ref, o_ref, h_ref, acc_ref):
    x1 = _ffn_half_step(x_ref, gain_ref, wgu_ref, wd_ref, h_ref, acc_ref)
    e = jnp.dot(p_ref[...].astype(BF16), wproj_ref[...], preferred_element_type=F32)
    half_e = 0.5 * _rms(e, post_ref[...])
    base = x1 + half_e
    hp = _rms(x1, pgain_ref[...]).astype(BF16)
    part = 2 * V7X_MXU_DIM
    for c0 in range(0, x1.shape[1], part):
        cols = slice(c0, c0 + part)
        half_z = jnp.dot(hp, wgate_ref[:, cols], preferred_element_type=F32) + bgate_ref[:, cols]
        o_ref[:, cols] = base[:, cols] + half_e[:, cols] * jnp.tanh(half_z)


def _ffn(x2, gain, w_gu, w_down, ple=None):
    n_tok, d = x2.shape
    d_ff = w_down.shape[0]
    assert d_ff % FFN_CHUNK == 0 and n_tok % FFN_ROWS == 0 and w_gu.shape == (d, 2 * d_ff)
    row_spec = pl.BlockSpec((FFN_ROWS, d), lambda s: (s, 0))
    in_specs = [row_spec, _const_spec((1, d)), _const_spec(w_gu.shape), _const_spec(w_down.shape)]
    gate_half = jnp.where(jnp.arange(2 * d_ff) < d_ff, 0.5, 1.0).astype(F32)
    args = [x2, gain.reshape(1, d), (w_gu * gate_half).astype(BF16), w_down.astype(BF16)]
    if ple is not None:
        p2, pgain, w_gate, b_gate, w_proj, post_gain = ple
        in_specs += [pl.BlockSpec((FFN_ROWS, p2.shape[1]), lambda s: (s, 0)), _const_spec((1, d)),
                     _const_spec(w_gate.shape), _const_spec((1, d)), _const_spec(w_proj.shape),
                     _const_spec((1, d))]
        args += [p2, pgain.reshape(1, d), (0.5 * w_gate).astype(BF16), (0.5 * b_gate).reshape(1, d),
                 w_proj.astype(BF16), post_gain.reshape(1, d)]
    return pl.pallas_call(
        _ffn_kernel if ple is None else _ffn_ple_kernel,
        name="ffn" if ple is None else "ffn_ple",
        grid=(n_tok // FFN_ROWS,),
        in_specs=in_specs,
        out_specs=row_spec,
        out_shape=jax.ShapeDtypeStruct((n_tok, d), F32),
        scratch_shapes=[pltpu.VMEM((FFN_ROWS, d), BF16), pltpu.VMEM((FFN_ROWS, d), F32)],
        compiler_params=pltpu.CompilerParams(
            dimension_semantics=("arbitrary",), vmem_limit_bytes=VMEM_LIMIT),
    )(*args)


def _head_rms_cols(t, low, gain):
    heads_per_col = 2
    cols = []
    for c in range(t.shape[1] // V7X_LANES):
        sl = slice(c * V7X_LANES, (c + 1) * V7X_LANES)
        tc = t[:, sl]
        sq = tc * tc
        ss_low = jnp.sum(jnp.where(low, sq, 0.0), axis=-1, keepdims=True)
        ss_high = jnp.sum(jnp.where(low, 0.0, sq), axis=-1, keepdims=True)
        ms = jnp.where(low, ss_low, ss_high) * (heads_per_col / V7X_LANES)
        cols.append(tc * lax.rsqrt(ms + EPS) * gain[:, sl])
    return cols


def _dup_half(t, lane, half):
    r = pltpu.roll(t, half, axis=1)
    low = lane < half
    return jnp.where(low, t, r), jnp.where(low, r, t)


def _build_score_tables(bucket_ref, relb_ref, sinks_ref, tbl_ref, fill_ref):
    n_buckets, n_heads = relb_ref.shape

    def rows8(r, carry):
        r0 = pl.multiple_of(r * 8, 8)
        bidx = bucket_ref[pl.ds(r0, 8), :]
        slot0 = lax.broadcasted_iota(jnp.int32, bidx.shape, 1) == 0
        accs = [jnp.zeros(bidx.shape, F32) for _ in range(n_heads)]
        for b in range(n_buckets):
            hit = bidx == b
            accs = [jnp.where(hit, relb_ref[b, h], accs[h]) for h in range(n_heads)]
        for h in range(n_heads):
            tbl_ref[h, pl.ds(r0, 8), :] = accs[h]
            fill_ref[h, pl.ds(r0, 8), :] = jnp.where(slot0, sinks_ref[h], NEG_INF)
        return carry

    lax.fori_loop(0, WINDOW // 8, rows8, 0)


def _mixer_kernel(bucket_ref, relb_ref, sinks_ref, x_ref, gain_ref, win_ref, qg_ref, kg_ref,
                  poolw_ref, pscale_ref, wout_ref, o_ref,
                  tbl_ref, fill_ref, qs_ref, kctx_ref, vctx_ref, uext_ref, cat_ref):
    blk = WINDOW
    n_seq, rows, _ = x_ref.shape
    n_blk = rows // blk
    n_heads = tbl_ref.shape[0]
    n_kv = kctx_ref.shape[1]
    group = n_heads // n_kv
    head_dim = V7X_LANES // 2
    attn_w = n_heads * head_dim
    kv_w = n_kv * head_dim
    pool_w = cat_ref.shape[-1] - attn_w
    pool_g = pool_w // len(POOL_WINDOWS)
    j = pl.program_id(1)

    @pl.when((pl.program_id(0) == 0) & (j == 0))
    def _first_step():
        _build_score_tables(bucket_ref, relb_ref, sinks_ref, tbl_ref, fill_ref)

    @pl.when(j == 0)
    def _reset_carry():
        kctx_ref[:, :, 0:blk, :] = jnp.zeros((n_seq, n_kv, blk, V7X_LANES), BF16)
        vctx_ref[:, :, 0:blk, :] = jnp.zeros((n_seq, n_kv, blk, 2 * V7X_LANES), BF16)
        uext_ref[:, 0:MAX_POOL, :] = jnp.zeros((n_seq, MAX_POOL, pool_w), F32)

    lane = lax.broadcasted_iota(jnp.int32, (rows, V7X_LANES), 1)
    low = lane < head_dim
    ones = jnp.ones((rows, V7X_LANES), BF16)
    qi = lax.broadcasted_iota(jnp.int32, (blk, 2 * blk), 0)
    kj = lax.broadcasted_iota(jnp.int32, (blk, 2 * blk), 1)
    dist = qi + blk - kj
    band = (dist >= 0) & (dist < WINDOW)
    low_blk = lax.broadcasted_iota(jnp.int32, (blk, V7X_LANES), 1) < head_dim
    t_pos = j * rows + lax.broadcasted_iota(jnp.int32, (rows, 1), 0)
    top_row = lax.broadcasted_iota(jnp.int32, (BF16_SUBLANES, 2 * V7X_LANES), 0) == 0
    sink_row = (lax.broadcasted_iota(jnp.int32, (BF16_SUBLANES, 2 * V7X_LANES), 1)
                >= V7X_LANES).astype(F32)

    projs = []
    for s in range(n_seq):
        h = _rms(x_ref[s], gain_ref[...]).astype(BF16)
        projs.append(jnp.dot(h, win_ref[...], preferred_element_type=F32))

    for s, proj in enumerate(projs):
        q = proj[:, 0:attn_w]
        k = proj[:, attn_w:attn_w + kv_w]
        v = proj[:, attn_w + kv_w:attn_w + 2 * kv_w]
        uext_ref[s, MAX_POOL:MAX_POOL + rows, :] = proj[:, attn_w + 2 * kv_w:]
        for c, qc in enumerate(_head_rms_cols(q, low, qg_ref[...])):
            q_lo = jnp.where(low, qc, 0.0).astype(BF16)
            q_hi = jnp.where(low, 0.0, qc).astype(BF16)
            kvh, slot = divmod(2 * c, group)
            for b in range(n_blk):
                qs_ref[s, kvh, b, slot * blk:(slot + 1) * blk, :] = q_lo[b * blk:(b + 1) * blk]
                qs_ref[s, kvh, b, (slot + 1) * blk:(slot + 2) * blk, :] = q_hi[b * blk:(b + 1) * blk]
        (kn,) = _head_rms_cols(k, low, kg_ref[...])
        kv_pairs = zip(_dup_half(kn, lane, head_dim), _dup_half(v, lane, head_dim))
        for kvh, (kd, vd) in enumerate(kv_pairs):
            kctx_ref[s, kvh, blk:blk + rows, :] = kd.astype(BF16)
            vctx_ref[s, kvh, blk:blk + rows, 0:V7X_LANES] = vd.astype(BF16)
            vctx_ref[s, kvh, blk:blk + rows, V7X_LANES:] = ones

    def scores(s, b, kvh):
        kc = kctx_ref[s, kvh, b * blk:(b + 2) * blk, :]
        return lax.dot_general(qs_ref[s, kvh, b], kc, (((1,), (1,)), ((), ())),
                               preferred_element_type=F32)

    def attend(sc, s, b, kvh):
        r0 = b * blk
        valid = band if b > 0 else band & ((kj >= blk) | (j > 0))
        es = []
        for i in range(group):
            head = kvh * group + i
            si = jnp.where(valid, sc[i * blk:(i + 1) * blk] + tbl_ref[head], fill_ref[head])
            m = jnp.max(si, axis=-1, keepdims=True)
            es.append(jnp.exp2(si - m).astype(BF16))
        top = vctx_ref[s, kvh, r0:r0 + BF16_SUBLANES, :].astype(F32)
        top = jnp.where(top_row, sink_row, top).astype(BF16)
        vc = jnp.concatenate([top, vctx_ref[s, kvh, r0 + BF16_SUBLANES:r0 + 2 * blk, :]], axis=0)
        pv = jnp.dot(jnp.concatenate(es, axis=0), vc, preferred_element_type=F32)
        outs = [pv[i * blk:(i + 1) * blk, 0:V7X_LANES] / pv[i * blk:(i + 1) * blk, V7X_LANES:]
                for i in range(group)]
        for pair in range(group // 2):
            col = (kvh * group + 2 * pair) * head_dim
            both = jnp.where(low_blk, outs[2 * pair], outs[2 * pair + 1])
            cat_ref[s, r0:r0 + blk, col:col + V7X_LANES] = both.astype(BF16)

    items = [(s, b, kvh) for b in range(n_blk) for kvh in range(n_kv) for s in range(n_seq)]
    sc_next = scores(*items[0])
    for n, item in enumerate(items):
        sc = sc_next
        if n + 1 < len(items):
            sc_next = scores(*items[n + 1])
        attend(sc, *item)

    for g, w in enumerate(POOL_WINDOWS):
        cols = slice(g * pool_g, (g + 1) * pool_g)
        cnt = jnp.minimum(t_pos + 1, w).astype(F32)
        for s in range(n_seq):
            total = uext_ref[s, :, cols]
            shift = 1
            while shift < w:
                total = total + pltpu.roll(total, shift, axis=0)
                shift *= 2
            d = total[MAX_POOL:] * (1.0 / cnt) - uext_ref[s, MAX_POOL:, cols]
            y = jnp.dot(d.astype(BF16), poolw_ref[g], preferred_element_type=F32)
            cat_ref[s, :, attn_w + g * pool_g:attn_w + (g + 1) * pool_g] = (
                y * pscale_ref[:, cols]).astype(BF16)

    for s in range(n_seq):
        o_ref[s] = x_ref[s] + jnp.dot(cat_ref[s], wout_ref[...], preferred_element_type=F32)
    kctx_ref[:, :, 0:blk, :] = kctx_ref[:, :, rows:rows + blk, :]
    vctx_ref[:, :, 0:blk, :] = vctx_ref[:, :, rows:rows + blk, :]
    uext_ref[:, 0:MAX_POOL, :] = uext_ref[:, rows:rows + MAX_POOL, :]


def _t5_bucket_table(n_buckets):
    qi = jnp.arange(WINDOW)[:, None]
    kj = jnp.arange(2 * WINDOW)[None, :]
    n = jnp.maximum(qi + WINDOW - kj, 0)
    max_exact = n_buckets // 2
    nf = jnp.maximum(n, 1).astype(F32)
    large = max_exact + (jnp.log(nf / max_exact) / math.log(MAX_DISTANCE / max_exact)
                         * (n_buckets - max_exact)).astype(jnp.int32)
    large = jnp.minimum(large, n_buckets - 1)
    return jnp.where(n < max_exact, n, large).astype(jnp.int32)


def _mixer(x2, seq, gain, w_in, q_norm, k_norm, rel_bias, sinks, pool_w, pool_scale, w_out):
    n_tok, d = x2.shape
    head_dim = q_norm.shape[0]
    n_buckets, n_heads = rel_bias.shape
    n_groups, pool_g, _ = pool_w.shape
    attn_w = n_heads * head_dim
    pool_width = n_groups * pool_g
    kv_w = (w_in.shape[1] - attn_w - pool_width) // 2
    n_kv = kv_w // head_dim
    group = n_heads // n_kv
    rows = TILE_ROWS
    n_seq = MIX_SEQS
    batch = n_tok // seq
    assert n_groups == len(POOL_WINDOWS) and pool_g == V7X_LANES
    assert 2 * head_dim == V7X_LANES and kv_w == V7X_LANES and group % 2 == 0
    assert seq % rows == 0 and rows % WINDOW == 0 and w_out.shape[0] == attn_w + pool_width
    assert batch % n_seq == 0

    q_gain = jnp.tile(q_norm, n_heads) * (LOG2E / math.sqrt(head_dim))

    row_spec = pl.BlockSpec((None, n_seq, rows, d), lambda b, j: (b, 0, j, 0))
    smem = pl.BlockSpec(memory_space=pltpu.SMEM)
    out = pl.pallas_call(
        _mixer_kernel,
        name="mixer",
        grid=(batch // n_seq, seq // rows),
        in_specs=[
            _const_spec((WINDOW, 2 * WINDOW)), smem, smem,
            row_spec, _const_spec((1, d)), _const_spec(w_in.shape),
            _const_spec((1, attn_w)), _const_spec((1, kv_w)),
            _const_spec(pool_w.shape), _const_spec((1, pool_width)), _const_spec(w_out.shape),
        ],
        out_specs=row_spec,
        out_shape=jax.ShapeDtypeStruct((batch // n_seq, n_seq, seq, d), F32),
        scratch_shapes=[
            pltpu.VMEM((n_heads, WINDOW, 2 * WINDOW), F32),
            pltpu.VMEM((n_heads, WINDOW, 2 * WINDOW), F32),
            pltpu.VMEM((n_seq, n_kv, rows // WINDOW, group * WINDOW, V7X_LANES), BF16),
            pltpu.VMEM((n_seq, n_kv, WINDOW + rows, V7X_LANES), BF16),
            pltpu.VMEM((n_seq, n_kv, WINDOW + rows, 2 * V7X_LANES), BF16),
            pltpu.VMEM((n_seq, MAX_POOL + rows, pool_width), F32),
            pltpu.VMEM((n_seq, rows, attn_w + pool_width), BF16),
        ],
        compiler_params=pltpu.CompilerParams(
            dimension_semantics=("arbitrary", "arbitrary"), vmem_limit_bytes=VMEM_LIMIT),
    )(_t5_bucket_table(n_buckets), rel_bias * LOG2E, sinks * LOG2E,
      x2.reshape(batch // n_seq, n_seq, seq, d), gain.reshape(1, d), w_in.astype(BF16),
      q_gain.reshape(1, attn_w), jnp.tile(k_norm, n_kv).reshape(1, kv_w),
      pool_w.astype(BF16), pool_scale.reshape(1, pool_width), w_out.astype(BF16))
    return out.reshape(n_tok, d)


def kernel(x, p, ffn1_norm, ffn1_w_gu, ffn1_w_down, mix_norm, w_in, q_norm, k_norm, rel_bias,
           sinks, pool_w, pool_scale, w_out, ffn2_norm, ffn2_w_gu, ffn2_w_down, ple_norm,
           ple_w_gate, ple_b_gate, ple_w_proj, ple_post_norm):
    batch, seq, d = x.shape
    x2 = x.reshape(batch * seq, d)
    for i in range(p.shape[0]):
        x2 = _ffn(x2, ffn1_norm[i], ffn1_w_gu[i], ffn1_w_down[i])
        x2 = _mixer(x2, seq, mix_norm[i], w_in[i], q_norm[i], k_norm[i], rel_bias, sinks[i],
                    pool_w[i], pool_scale[i], w_out[i])
        x2 = _ffn(x2, ffn2_norm[i], ffn2_w_gu[i], ffn2_w_down[i],
                  ple=(p[i].reshape(batch * seq, -1), ple_norm[i], ple_w_gate[i], ple_b_gate[i],
                       ple_w_proj[i], ple_post_norm[i]))
    return x2.reshape(batch, seq, d)
```

```python
import math

import jax
import jax.numpy as jnp
from jax import lax
from jax.experimental import pallas as pl
from jax.experimental.pallas import tpu as pltpu

F32 = jnp.float32
BF16 = jnp.bfloat16

WINDOW = 128
MAX_DISTANCE = 128
POOL_WINDOWS = (2, 4, 8, 16)
MACARON_WEIGHT = 0.5
EPS = 1e-6
NEG_INF = -1e30
LOG2E = math.log2(math.e)

V7X_LANES = 128
V7X_MXU_DIM = 256
BF16_SUBLANES = 16

TILE_ROWS = 512
FFN_ROWS = 1024
MIX_SEQS = 2
FFN_CHUNK = V7X_MXU_DIM
MAX_POOL = max(POOL_WINDOWS)
FFN_VMEM_LIMIT = 56 * 1024 * 1024
MIX_VMEM_LIMIT = 48 * 1024 * 1024


def _rms(x, gain):
    ms = jnp.mean(x * x, axis=-1, keepdims=True)
    return x * lax.rsqrt(ms + EPS) * gain


def _silu_of_half(h):
    return h + h * jnp.tanh(h)


def _const_spec(shape):
    zeros = (0,) * len(shape)
    return pl.BlockSpec(shape, lambda *_: zeros, pipeline_mode=pl.Buffered(1))


def _ffn_half_step(x_ref, gain_ref, wgu_ref, wd_ref, h_ref, acc_ref):
    h_ref[...] = _rms(x_ref[...], gain_ref[...]).astype(BF16)
    d_ff = wd_ref.shape[0]
    n_chunks = d_ff // FFN_CHUNK
    for c in range(n_chunks):
        cols = slice(c * FFN_CHUNK, (c + 1) * FFN_CHUNK)
        ucols = slice(d_ff + c * FFN_CHUNK, d_ff + (c + 1) * FFN_CHUNK)
        half_g = jnp.dot(h_ref[...], wgu_ref[:, cols], preferred_element_type=F32)
        u = jnp.dot(h_ref[...], wgu_ref[:, ucols], preferred_element_type=F32)
        act = _silu_of_half(half_g) * u
        down = jnp.dot(act.astype(BF16), wd_ref[cols, :], preferred_element_type=F32)
        if c == 0:
            acc_ref[...] = down
        elif c < n_chunks - 1:
            acc_ref[...] += down
    return x_ref[...] + MACARON_WEIGHT * (acc_ref[...] + down)


def _ffn_kernel(x_ref, gain_ref, wgu_ref, wd_ref, o_ref, h_ref, acc_ref):
    o_ref[...] = _ffn_half_step(x_ref, gain_ref, wgu_ref, wd_ref, h_ref, acc_ref)


def _ffn_ple_kernel(x_ref, gain_ref, wgu_ref, wd_ref, p_ref, pgain_ref, wgate_ref, bgate_ref,
                    wproj_ref, post_ref, o_ref, h_ref, acc_ref):
    x1 = _ffn_half_step(x_ref, gain_ref, wgu_ref, wd_ref, h_ref, acc_ref)
    e = jnp.dot(p_ref[...].astype(BF16), wproj_ref[...], preferred_element_type=F32)
    half_e = 0.5 * _rms(e, post_ref[...])
    base = x1 + half_e
    hp = _rms(x1, pgain_ref[...]).astype(BF16)
    part = 2 * V7X_MXU_DIM
    for c0 in range(0, x1.shape[1], part):
        cols = slice(c0, c0 + part)
        half_z = jnp.dot(hp, wgate_ref[:, cols], preferred_element_type=F32) + bgate_ref[:, cols]
        o_ref[:, cols] = base[:, cols] + half_e[:, cols] * jnp.tanh(half_z)


def _ffn(x2, gain, w_gu, w_down, ple=None):
    n_tok, d = x2.shape
    d_ff = w_down.shape[0]
    assert d_ff % FFN_CHUNK == 0 and n_tok % FFN_ROWS == 0 and w_gu.shape == (d, 2 * d_ff)
    row_spec = pl.BlockSpec((FFN_ROWS, d), lambda s: (s, 0))
    in_specs = [row_spec, _const_spec((1, d)), _const_spec(w_gu.shape), _const_spec(w_down.shape)]
    gate_half = jnp.where(jnp.arange(2 * d_ff) < d_ff, 0.5, 1.0).astype(F32)
    args = [x2, gain.reshape(1, d), (w_gu * gate_half).astype(BF16), w_down.astype(BF16)]
    if ple is not None:
        p2, pgain, w_gate, b_gate, w_proj, post_gain = ple
        in_specs += [pl.BlockSpec((FFN_ROWS, p2.shape[1]), lambda s: (s, 0)), _const_spec((1, d)),
                     _const_spec(w_gate.shape), _const_spec((1, d)), _const_spec(w_proj.shape),
                     _const_spec((1, d))]
        args += [p2, pgain.reshape(1, d), (0.5 * w_gate).astype(BF16), (0.5 * b_gate).reshape(1, d),
                 w_proj.astype(BF16), post_gain.reshape(1, d)]
    return pl.pallas_call(
        _ffn_kernel if ple is None else _ffn_ple_kernel,
        name="ffn" if ple is None else "ffn_ple",
        grid=(n_tok // FFN_ROWS,),
        in_specs=in_specs,
        out_specs=row_spec,
        out_shape=jax.ShapeDtypeStruct((n_tok, d), F32),
        scratch_shapes=[pltpu.VMEM((FFN_ROWS, d), BF16), pltpu.VMEM((FFN_ROWS, d), F32)],
        compiler_params=pltpu.CompilerParams(
            dimension_semantics=("arbitrary",), vmem_limit_bytes=FFN_VMEM_LIMIT),
    )(*args)


def _head_rms_cols(t, low, gain):
    heads_per_col = 2
    cols = []
    for c in range(t.shape[1] // V7X_LANES):
        sl = slice(c * V7X_LANES, (c + 1) * V7X_LANES)
        tc = t[:, sl]
        sq = tc * tc
        ss_low = jnp.sum(jnp.where(low, sq, 0.0), axis=-1, keepdims=True)
        ss_high = jnp.sum(jnp.where(low, 0.0, sq), axis=-1, keepdims=True)
        ms = jnp.where(low, ss_low, ss_high) * (heads_per_col / V7X_LANES)
        cols.append(tc * lax.rsqrt(ms + EPS) * gain[:, sl])
    return cols


def _dup_half(t, lane, half):
    r = pltpu.roll(t, half, axis=1)
    low = lane < half
    return jnp.where(low, t, r), jnp.where(low, r, t)


def _build_score_tables(bucket_ref, relb_ref, sinks_ref, tbl_ref, fill_ref):
    n_buckets, n_heads = relb_ref.shape

    def rows8(r, carry):
        r0 = pl.multiple_of(r * 8, 8)
        bidx = bucket_ref[pl.ds(r0, 8), :]
        slot0 = lax.broadcasted_iota(jnp.int32, bidx.shape, 1) == 0
        accs = [jnp.zeros(bidx.shape, F32) for _ in range(n_heads)]
        for b in range(n_buckets):
            hit = bidx == b
            accs = [jnp.where(hit, relb_ref[b, h], accs[h]) for h in range(n_heads)]
        for h in range(n_heads):
            tbl_ref[h, pl.ds(r0, 8), :] = accs[h]
            fill_ref[h, pl.ds(r0, 8), :] = jnp.where(slot0, sinks_ref[h], NEG_INF)
        return carry

    lax.fori_loop(0, WINDOW // 8, rows8, 0)


def _mixer_kernel(bucket_ref, relb_ref, sinks_ref, x_ref, gain_ref, win_ref, qg_ref, kg_ref,
                  poolw_ref, pscale_ref, wout_ref, o_ref,
                  tbl_ref, fill_ref, qs_ref, kctx_ref, vctx_ref, uext_ref, cat_ref):
    blk = WINDOW
    n_seq, rows, _ = x_ref.shape
    n_blk = rows // blk
    n_heads = tbl_ref.shape[0]
    n_kv = kctx_ref.shape[1]
    group = n_heads // n_kv
    head_dim = V7X_LANES // 2
    attn_w = n_heads * head_dim
    kv_w = n_kv * head_dim
    pool_w = cat_ref.shape[-1] - attn_w
    pool_g = pool_w // len(POOL_WINDOWS)
    j = pl.program_id(1)

    @pl.when((pl.program_id(0) == 0) & (j == 0))
    def _first_step():
        _build_score_tables(bucket_ref, relb_ref, sinks_ref, tbl_ref, fill_ref)

    @pl.when(j == 0)
    def _reset_carry():
        kctx_ref[:, :, 0:blk, :] = jnp.zeros((n_seq, n_kv, blk, V7X_LANES), BF16)
        vctx_ref[:, :, 0:blk, :] = jnp.zeros((n_seq, n_kv, blk, 2 * V7X_LANES), BF16)
        uext_ref[:, 0:MAX_POOL, :] = jnp.zeros((n_seq, MAX_POOL, pool_w), F32)

    lane = lax.broadcasted_iota(jnp.int32, (rows, V7X_LANES), 1)
    low = lane < head_dim
    ones = jnp.ones((rows, V7X_LANES), BF16)
    qi = lax.broadcasted_iota(jnp.int32, (blk, 2 * blk), 0)
    kj = lax.broadcasted_iota(jnp.int32, (blk, 2 * blk), 1)
    dist = qi + blk - kj
    band = (dist >= 0) & (dist < WINDOW)
    low_blk = lax.broadcasted_iota(jnp.int32, (blk, V7X_LANES), 1) < head_dim
    t_pos = j * rows + lax.broadcasted_iota(jnp.int32, (rows, 1), 0)
    top_row = lax.broadcasted_iota(jnp.int32, (BF16_SUBLANES, 2 * V7X_LANES), 0) == 0
    sink_row = (lax.broadcasted_iota(jnp.int32, (BF16_SUBLANES, 2 * V7X_LANES), 1)
                >= V7X_LANES).astype(F32)

    projs = []
    for s in range(n_seq):
        h = _rms(x_ref[s], gain_ref[...]).astype(BF16)
        projs.append(jnp.dot(h, win_ref[...], preferred_element_type=F32))

    for s, proj in enumerate(projs):
        q = proj[:, 0:attn_w]
        k = proj[:, attn_w:attn_w + kv_w]
        v = proj[:, attn_w + kv_w:attn_w + 2 * kv_w]
        uext_ref[s, MAX_POOL:MAX_POOL + rows, :] = proj[:, attn_w + 2 * kv_w:]
        for c, qc in enumerate(_head_rms_cols(q, low, qg_ref[...])):
            q_lo = jnp.where(low, qc, 0.0).astype(BF16)
            q_hi = jnp.where(low, 0.0, qc).astype(BF16)
            kvh, slot = divmod(2 * c, group)
            for b in range(n_blk):
                qs_ref[s, kvh, b, slot * blk:(slot + 1) * blk, :] = q_lo[b * blk:(b + 1) * blk]
                qs_ref[s, kvh, b, (slot + 1) * blk:(slot + 2) * blk, :] = q_hi[b * blk:(b + 1) * blk]
        (kn,) = _head_rms_cols(k, low, kg_ref[...])
        kv_pairs = zip(_dup_half(kn, lane, head_dim), _dup_half(v, lane, head_dim))
        for kvh, (kd, vd) in enumerate(kv_pairs):
            kctx_ref[s, kvh, blk:blk + rows, :] = kd.astype(BF16)
            vctx_ref[s, kvh, blk:blk + rows, 0:V7X_LANES] = vd.astype(BF16)
            vctx_ref[s, kvh, blk:blk + rows, V7X_LANES:] = ones

    def scores(s, b, kvh):
        kc = kctx_ref[s, kvh, b * blk:(b + 2) * blk, :]
        return lax.dot_general(qs_ref[s, kvh, b], kc, (((1,), (1,)), ((), ())),
                               preferred_element_type=F32)

    def attend(sc, s, b, kvh):
        r0 = b * blk
        valid = band if b > 0 else band & ((kj >= blk) | (j > 0))
        es = []
        for i in range(group):
            head = kvh * group + i
            si = jnp.where(valid, sc[i * blk:(i + 1) * blk] + tbl_ref[head], fill_ref[head])
            m = jnp.max(si, axis=-1, keepdims=True)
            es.append(jnp.exp2(si - m).astype(BF16))
        top = vctx_ref[s, kvh, r0:r0 + BF16_SUBLANES, :].astype(F32)
        top = jnp.where(top_row, sink_row, top).astype(BF16)
        vc = jnp.concatenate([top, vctx_ref[s, kvh, r0 + BF16_SUBLANES:r0 + 2 * blk, :]], axis=0)
        pv = jnp.dot(jnp.concatenate(es, axis=0), vc, preferred_element_type=F32)
        outs = [pv[i * blk:(i + 1) * blk, 0:V7X_LANES] / pv[i * blk:(i + 1) * blk, V7X_LANES:]
                for i in range(group)]
        for pair in range(group // 2):
            col = (kvh * group + 2 * pair) * head_dim
            both = jnp.where(low_blk, outs[2 * pair], outs[2 * pair + 1])
            cat_ref[s, r0:r0 + blk, col:col + V7X_LANES] = both.astype(BF16)

    items = [(s, b, kvh) for b in range(n_blk) for kvh in range(n_kv) for s in range(n_seq)]
    sc_next = scores(*items[0])
    for n, item in enumerate(items):
        sc = sc_next
        if n + 1 < len(items):
            sc_next = scores(*items[n + 1])
        attend(sc, *item)

    for g, w in enumerate(POOL_WINDOWS):
        cols = slice(g * pool_g, (g + 1) * pool_g)
        cnt = jnp.minimum(t_pos + 1, w).astype(F32)
        for s in range(n_seq):
            total = uext_ref[s, :, cols]
            shift = 1
            while shift < w:
                total = total + pltpu.roll(total, shift, axis=0)
                shift *= 2
            d = total[MAX_POOL:] * (1.0 / cnt) - uext_ref[s, MAX_POOL:, cols]
            y = jnp.dot(d.astype(BF16), poolw_ref[g], preferred_element_type=F32)
            cat_ref[s, :, attn_w + g * pool_g:attn_w + (g + 1) * pool_g] = (
                y * pscale_ref[:, cols]).astype(BF16)

    for s in range(n_seq):
        o_ref[s] = x_ref[s] + jnp.dot(cat_ref[s], wout_ref[...], preferred_element_type=F32)
    kctx_ref[:, :, 0:blk, :] = kctx_ref[:, :, rows:rows + blk, :]
    vctx_ref[:, :, 0:blk, :] = vctx_ref[:, :, rows:rows + blk, :]
    uext_ref[:, 0:MAX_POOL, :] = uext_ref[:, rows:rows + MAX_POOL, :]


def _t5_bucket_table(n_buckets):
    qi = jnp.arange(WINDOW)[:, None]
    kj = jnp.arange(2 * WINDOW)[None, :]
    n = jnp.maximum(qi + WINDOW - kj, 0)
    max_exact = n_buckets // 2
    nf = jnp.maximum(n, 1).astype(F32)
    large = max_exact + (jnp.log(nf / max_exact) / math.log(MAX_DISTANCE / max_exact)
                         * (n_buckets - max_exact)).astype(jnp.int32)
    large = jnp.minimum(large, n_buckets - 1)
    return jnp.where(n < max_exact, n, large).astype(jnp.int32)


def _mixer(x2, seq, gain, w_in, q_norm, k_norm, rel_bias, sinks, pool_w, pool_scale, w_out):
    n_tok, d = x2.shape
    head_dim = q_norm.shape[0]
    n_buckets, n_heads = rel_bias.shape
    n_groups, pool_g, _ = pool_w.shape
    attn_w = n_heads * head_dim
    pool_width = n_groups * pool_g
    kv_w = (w_in.shape[1] - attn_w - pool_width) // 2
    n_kv = kv_w // head_dim
    group = n_heads // n_kv
    rows = TILE_ROWS
    n_seq = MIX_SEQS
    batch = n_tok // seq
    assert n_groups == len(POOL_WINDOWS) and pool_g == V7X_LANES
    assert 2 * head_dim == V7X_LANES and kv_w == V7X_LANES and group % 2 == 0
    assert seq % rows == 0 and rows % WINDOW == 0 and w_out.shape[0] == attn_w + pool_width
    assert batch % n_seq == 0

    q_gain = jnp.tile(q_norm, n_heads) * (LOG2E / math.sqrt(head_dim))

    row_spec = pl.BlockSpec((None, n_seq, rows, d), lambda b, j: (b, 0, j, 0))
    smem = pl.BlockSpec(memory_space=pltpu.SMEM)
    out = pl.pallas_call(
        _mixer_kernel,
        name="mixer",
        grid=(batch // n_seq, seq // rows),
        in_specs=[
            _const_spec((WINDOW, 2 * WINDOW)), smem, smem,
            row_spec, _const_spec((1, d)), _const_spec(w_in.shape),
            _const_spec((1, attn_w)), _const_spec((1, kv_w)),
            _const_spec(pool_w.shape), _const_spec((1, pool_width)), _const_spec(w_out.shape),
        ],
        out_specs=row_spec,
        out_shape=jax.ShapeDtypeStruct((batch // n_seq, n_seq, seq, d), F32),
        scratch_shapes=[
            pltpu.VMEM((n_heads, WINDOW, 2 * WINDOW), F32),
            pltpu.VMEM((n_heads, WINDOW, 2 * WINDOW), F32),
            pltpu.VMEM((n_seq, n_kv, rows // WINDOW, group * WINDOW, V7X_LANES), BF16),
            pltpu.VMEM((n_seq, n_kv, WINDOW + rows, V7X_LANES), BF16),
            pltpu.VMEM((n_seq, n_kv, WINDOW + rows, 2 * V7X_LANES), BF16),
            pltpu.VMEM((n_seq, MAX_POOL + rows, pool_width), F32),
            pltpu.VMEM((n_seq, rows, attn_w + pool_width), BF16),
        ],
        compiler_params=pltpu.CompilerParams(
            dimension_semantics=("arbitrary", "arbitrary"), vmem_limit_bytes=MIX_VMEM_LIMIT),
    )(_t5_bucket_table(n_buckets), rel_bias * LOG2E, sinks * LOG2E,
      x2.reshape(batch // n_seq, n_seq, seq, d), gain.reshape(1, d), w_in.astype(BF16),
      q_gain.reshape(1, attn_w), jnp.tile(k_norm, n_kv).reshape(1, kv_w),
      pool_w.astype(BF16), pool_scale.reshape(1, pool_width), w_out.astype(BF16))
    return out.reshape(n_tok, d)


def kernel(x, p, ffn1_norm, ffn1_w_gu, ffn1_w_down, mix_norm, w_in, q_norm, k_norm, rel_bias,
           sinks, pool_w, pool_scale, w_out, ffn2_norm, ffn2_w_gu, ffn2_w_down, ple_norm,
           ple_w_gate, ple_b_gate, ple_w_proj, ple_post_norm):
    batch, seq, d = x.shape
    x2 = x.reshape(batch * seq, d)
    for i in range(p.shape[0]):
        x2 = _ffn(x2, ffn1_norm[i], ffn1_w_gu[i], ffn1_w_down[i])
        x2 = _mixer(x2, seq, mix_norm[i], w_in[i], q_norm[i], k_norm[i], rel_bias, sinks[i],
                    pool_w[i], pool_scale[i], w_out[i])
        x2 = _ffn(x2, ffn2_norm[i], ffn2_w_gu[i], ffn2_w_down[i],
                  ple=(p[i].reshape(batch * seq, -1), ple_norm[i], ple_w_gate[i], ple_b_gate[i],
                       ple_w_proj[i], ple_post_norm[i]))
    return x2.reshape(batch, seq, d)
```

```python
import math

import jax
import jax.numpy as jnp
from jax import lax
from jax.experimental import pallas as pl
from jax.experimental.pallas import tpu as pltpu

F32 = jnp.float32
BF16 = jnp.bfloat16

WINDOW = 128
MAX_DISTANCE = 128
POOL_WINDOWS = (2, 4, 8, 16)
MACARON_WEIGHT = 0.5
EPS = 1e-6
NEG_INF = -1e30
LOG2E = math.log2(math.e)

V7X_LANES = 128
V7X_MXU_DIM = 256
F32_SUBLANES = 8
BF16_SUBLANES = 16

TILE_ROWS = 512
FFN_ROWS = 1024
MIX_SEQS = 2
FFN_CHUNK = V7X_MXU_DIM
MAX_POOL = max(POOL_WINDOWS)
FFN_VMEM_LIMIT = 56 * 1024 * 1024
MIX_VMEM_LIMIT = 48 * 1024 * 1024


def _rms(x, gain):
    ms = jnp.mean(x * x, axis=-1, keepdims=True)
    return x * lax.rsqrt(ms + EPS) * gain


def _silu_of_half(h):
    return h + h * jnp.tanh(h)


def _const_spec(shape):
    zeros = (0,) * len(shape)
    return pl.BlockSpec(shape, lambda *_: zeros, pipeline_mode=pl.Buffered(1))


def _ffn_half_step(x_ref, gain_ref, wgu_ref, wd_ref, h_ref, acc_ref):
    h_ref[...] = _rms(x_ref[...], gain_ref[...]).astype(BF16)
    d_ff = wd_ref.shape[0]
    n_chunks = d_ff // FFN_CHUNK
    for c in range(n_chunks):
        cols = slice(c * FFN_CHUNK, (c + 1) * FFN_CHUNK)
        ucols = slice(d_ff + c * FFN_CHUNK, d_ff + (c + 1) * FFN_CHUNK)
        half_g = jnp.dot(h_ref[...], wgu_ref[:, cols], preferred_element_type=F32)
        u = jnp.dot(h_ref[...], wgu_ref[:, ucols], preferred_element_type=F32)
        act = _silu_of_half(half_g) * u
        down = jnp.dot(act.astype(BF16), wd_ref[cols, :], preferred_element_type=F32)
        if c == 0:
            acc_ref[...] = down
        elif c < n_chunks - 1:
            acc_ref[...] += down
    return x_ref[...] + MACARON_WEIGHT * (acc_ref[...] + down)


def _ffn_kernel(x_ref, gain_ref, wgu_ref, wd_ref, o_ref, h_ref, acc_ref):
    o_ref[...] = _ffn_half_step(x_ref, gain_ref, wgu_ref, wd_ref, h_ref, acc_ref)


def _ffn_ple_kernel(x_ref, gain_ref, wgu_ref, wd_ref, p_ref, pgain_ref, wgate_ref, bgate_ref,
                    wproj_ref, post_ref, o_ref, h_ref, acc_ref):
    x1 = _ffn_half_step(x_ref, gain_ref, wgu_ref, wd_ref, h_ref, acc_ref)
    e = jnp.dot(p_ref[...].astype(BF16), wproj_ref[...], preferred_element_type=F32)
    half_e = 0.5 * _rms(e, post_ref[...])
    base = x1 + half_e
    hp = _rms(x1, pgain_ref[...]).astype(BF16)
    part = 2 * V7X_MXU_DIM
    for c0 in range(0, x1.shape[1], part):
        cols = slice(c0, c0 + part)
        half_z = jnp.dot(hp, wgate_ref[:, cols], preferred_element_type=F32) + bgate_ref[:, cols]
        o_ref[:, cols] = base[:, cols] + half_e[:, cols] * jnp.tanh(half_z)


def _ffn(x2, gain, w_gu, w_down, ple=None):
    n_tok, d = x2.shape
    d_ff = w_down.shape[0]
    assert d_ff % FFN_CHUNK == 0 and n_tok % FFN_ROWS == 0 and w_gu.shape == (d, 2 * d_ff)
    row_spec = pl.BlockSpec((FFN_ROWS, d), lambda s: (s, 0))
    in_specs = [row_spec, _const_spec((1, d)), _const_spec(w_gu.shape), _const_spec(w_down.shape)]
    gate_half = jnp.where(jnp.arange(2 * d_ff) < d_ff, 0.5, 1.0).astype(F32)
    args = [x2, gain.reshape(1, d), (w_gu * gate_half).astype(BF16), w_down.astype(BF16)]
    if ple is not None:
        p2, pgain, w_gate, b_gate, w_proj, post_gain = ple
        in_specs += [pl.BlockSpec((FFN_ROWS, p2.shape[1]), lambda s: (s, 0)), _const_spec((1, d)),
                     _const_spec(w_gate.shape), _const_spec((1, d)), _const_spec(w_proj.shape),
                     _const_spec((1, d))]
        args += [p2, pgain.reshape(1, d), (0.5 * w_gate).astype(BF16), (0.5 * b_gate).reshape(1, d),
                 w_proj.astype(BF16), post_gain.reshape(1, d)]
    return pl.pallas_call(
        _ffn_kernel if ple is None else _ffn_ple_kernel,
        name="ffn" if ple is None else "ffn_ple",
        grid=(n_tok // FFN_ROWS,),
        in_specs=in_specs,
        out_specs=row_spec,
        out_shape=jax.ShapeDtypeStruct((n_tok, d), F32),
        scratch_shapes=[pltpu.VMEM((FFN_ROWS, d), BF16), pltpu.VMEM((FFN_ROWS, d), F32)],
        compiler_params=pltpu.CompilerParams(
            dimension_semantics=("arbitrary",), vmem_limit_bytes=FFN_VMEM_LIMIT),
    )(*args)


def _head_rms_cols(t, low, gain):
    heads_per_col = 2
    cols = []
    for c in range(t.shape[1] // V7X_LANES):
        sl = slice(c * V7X_LANES, (c + 1) * V7X_LANES)
        tc = t[:, sl]
        sq = tc * tc
        ss_low = jnp.sum(jnp.where(low, sq, 0.0), axis=-1, keepdims=True)
        ss_high = jnp.sum(jnp.where(low, 0.0, sq), axis=-1, keepdims=True)
        ms = jnp.where(low, ss_low, ss_high) * (heads_per_col / V7X_LANES)
        cols.append(tc * lax.rsqrt(ms + EPS) * gain[:, sl])
    return cols


def _dup_half(t, lane, half):
    r = pltpu.roll(t, half, axis=1)
    low = lane < half
    return jnp.where(low, t, r), jnp.where(low, r, t)


def _build_score_tables(bucket_ref, relb_ref, sinks_ref, tbl_ref, fill_ref):
    n_buckets, n_heads = relb_ref.shape
    step = F32_SUBLANES

    def row_group(r, carry):
        r0 = pl.multiple_of(r * step, step)
        bidx = bucket_ref[pl.ds(r0, step), :]
        slot0 = lax.broadcasted_iota(jnp.int32, bidx.shape, 1) == 0
        accs = [jnp.zeros(bidx.shape, F32) for _ in range(n_heads)]
        for b in range(n_buckets):
            hit = bidx == b
            accs = [jnp.where(hit, relb_ref[b, h], accs[h]) for h in range(n_heads)]
        for h in range(n_heads):
            tbl_ref[h, pl.ds(r0, step), :] = accs[h]
            fill_ref[h, pl.ds(r0, step), :] = jnp.where(slot0, sinks_ref[h], NEG_INF)
        return carry

    lax.fori_loop(0, WINDOW // step, row_group, 0)


def _mixer_kernel(bucket_ref, relb_ref, sinks_ref, x_ref, gain_ref, win_ref, qg_ref, kg_ref,
                  poolw_ref, pscale_ref, wout_ref, o_ref,
                  tbl_ref, fill_ref, qs_ref, kctx_ref, vctx_ref, uext_ref, cat_ref):
    blk = WINDOW
    n_seq, rows, _ = x_ref.shape
    n_blk = rows // blk
    n_heads = tbl_ref.shape[0]
    n_kv = kctx_ref.shape[1]
    group = n_heads // n_kv
    head_dim = V7X_LANES // 2
    attn_w = n_heads * head_dim
    kv_w = n_kv * head_dim
    pool_w = cat_ref.shape[-1] - attn_w
    pool_g = pool_w // len(POOL_WINDOWS)
    j = pl.program_id(1)

    @pl.when((pl.program_id(0) == 0) & (j == 0))
    def _first_step():
        _build_score_tables(bucket_ref, relb_ref, sinks_ref, tbl_ref, fill_ref)

    @pl.when(j == 0)
    def _reset_carry():
        kctx_ref[:, :, 0:blk, :] = jnp.zeros((n_seq, n_kv, blk, V7X_LANES), BF16)
        vctx_ref[:, :, 0:blk, :] = jnp.zeros((n_seq, n_kv, blk, 2 * V7X_LANES), BF16)
        uext_ref[:, 0:MAX_POOL, :] = jnp.zeros((n_seq, MAX_POOL, pool_w), F32)

    lane = lax.broadcasted_iota(jnp.int32, (rows, V7X_LANES), 1)
    low = lane < head_dim
    ones = jnp.ones((rows, V7X_LANES), BF16)
    qi = lax.broadcasted_iota(jnp.int32, (blk, 2 * blk), 0)
    kj = lax.broadcasted_iota(jnp.int32, (blk, 2 * blk), 1)
    dist = qi + blk - kj
    band = (dist >= 0) & (dist < WINDOW)
    low_blk = lax.broadcasted_iota(jnp.int32, (blk, V7X_LANES), 1) < head_dim
    t_pos = j * rows + lax.broadcasted_iota(jnp.int32, (rows, 1), 0)
    top_row = lax.broadcasted_iota(jnp.int32, (BF16_SUBLANES, 2 * V7X_LANES), 0) == 0
    sink_row = (lax.broadcasted_iota(jnp.int32, (BF16_SUBLANES, 2 * V7X_LANES), 1)
                >= V7X_LANES).astype(F32)

    projs = []
    for s in range(n_seq):
        h = _rms(x_ref[s], gain_ref[...]).astype(BF16)
        projs.append(jnp.dot(h, win_ref[...], preferred_element_type=F32))

    for s, proj in enumerate(projs):
        q = proj[:, 0:attn_w]
        k = proj[:, attn_w:attn_w + kv_w]
        v = proj[:, attn_w + kv_w:attn_w + 2 * kv_w]
        uext_ref[s, MAX_POOL:MAX_POOL + rows, :] = proj[:, attn_w + 2 * kv_w:]
        for c, qc in enumerate(_head_rms_cols(q, low, qg_ref[...])):
            q_lo = jnp.where(low, qc, 0.0).astype(BF16)
            q_hi = jnp.where(low, 0.0, qc).astype(BF16)
            kvh, slot = divmod(2 * c, group)
            for b in range(n_blk):
                qs_ref[s, kvh, b, slot * blk:(slot + 1) * blk, :] = q_lo[b * blk:(b + 1) * blk]
                qs_ref[s, kvh, b, (slot + 1) * blk:(slot + 2) * blk, :] = q_hi[b * blk:(b + 1) * blk]
        (kn,) = _head_rms_cols(k, low, kg_ref[...])
        kv_pairs = zip(_dup_half(kn, lane, head_dim), _dup_half(v, lane, head_dim))
        for kvh, (kd, vd) in enumerate(kv_pairs):
            kctx_ref[s, kvh, blk:blk + rows, :] = kd.astype(BF16)
            vctx_ref[s, kvh, blk:blk + rows, 0:V7X_LANES] = vd.astype(BF16)
            vctx_ref[s, kvh, blk:blk + rows, V7X_LANES:] = ones

    def scores(s, b, kvh):
        kc = kctx_ref[s, kvh, b * blk:(b + 2) * blk, :]
        return lax.dot_general(qs_ref[s, kvh, b], kc, (((1,), (1,)), ((), ())),
                               preferred_element_type=F32)

    def attend(sc, s, b, kvh):
        r0 = b * blk
        valid = band if b > 0 else band & ((kj >= blk) | (j > 0))
        es = []
        for i in range(group):
            head = kvh * group + i
            si = jnp.where(valid, sc[i * blk:(i + 1) * blk] + tbl_ref[head], fill_ref[head])
            m = jnp.max(si, axis=-1, keepdims=True)
            es.append(jnp.exp2(si - m).astype(BF16))
        top = vctx_ref[s, kvh, r0:r0 + BF16_SUBLANES, :].astype(F32)
        top = jnp.where(top_row, sink_row, top).astype(BF16)
        vc = jnp.concatenate([top, vctx_ref[s, kvh, r0 + BF16_SUBLANES:r0 + 2 * blk, :]], axis=0)
        pv = jnp.dot(jnp.concatenate(es, axis=0), vc, preferred_element_type=F32)
        outs = [pv[i * blk:(i + 1) * blk, 0:V7X_LANES] / pv[i * blk:(i + 1) * blk, V7X_LANES:]
                for i in range(group)]
        for pair in range(group // 2):
            col = (kvh * group + 2 * pair) * head_dim
            both = jnp.where(low_blk, outs[2 * pair], outs[2 * pair + 1])
            cat_ref[s, r0:r0 + blk, col:col + V7X_LANES] = both.astype(BF16)

    items = [(s, b, kvh) for b in range(n_blk) for kvh in range(n_kv) for s in range(n_seq)]
    sc_next = scores(*items[0])
    for n, item in enumerate(items):
        sc = sc_next
        if n + 1 < len(items):
            sc_next = scores(*items[n + 1])
        attend(sc, *item)

    for g, w in enumerate(POOL_WINDOWS):
        cols = slice(g * pool_g, (g + 1) * pool_g)
        cnt = jnp.minimum(t_pos + 1, w).astype(F32)
        for s in range(n_seq):
            total = uext_ref[s, :, cols]
            shift = 1
            while shift < w:
                total = total + pltpu.roll(total, shift, axis=0)
                shift *= 2
            d = total[MAX_POOL:] * (1.0 / cnt) - uext_ref[s, MAX_POOL:, cols]
            y = jnp.dot(d.astype(BF16), poolw_ref[g], preferred_element_type=F32)
            cat_ref[s, :, attn_w + g * pool_g:attn_w + (g + 1) * pool_g] = (
                y * pscale_ref[:, cols]).astype(BF16)

    for s in range(n_seq):
        o_ref[s] = x_ref[s] + jnp.dot(cat_ref[s], wout_ref[...], preferred_element_type=F32)
    kctx_ref[:, :, 0:blk, :] = kctx_ref[:, :, rows:rows + blk, :]
    vctx_ref[:, :, 0:blk, :] = vctx_ref[:, :, rows:rows + blk, :]
    uext_ref[:, 0:MAX_POOL, :] = uext_ref[:, rows:rows + MAX_POOL, :]


def _t5_bucket_table(n_buckets):
    qi = jnp.arange(WINDOW)[:, None]
    kj = jnp.arange(2 * WINDOW)[None, :]
    n = jnp.maximum(qi + WINDOW - kj, 0)
    max_exact = n_buckets // 2
    nf = jnp.maximum(n, 1).astype(F32)
    large = max_exact + (jnp.log(nf / max_exact) / math.log(MAX_DISTANCE / max_exact)
                         * (n_buckets - max_exact)).astype(jnp.int32)
    large = jnp.minimum(large, n_buckets - 1)
    return jnp.where(n < max_exact, n, large).astype(jnp.int32)


def _mixer(x2, seq, gain, w_in, q_norm, k_norm, rel_bias, sinks, pool_w, pool_scale, w_out):
    n_tok, d = x2.shape
    head_dim = q_norm.shape[0]
    n_buckets, n_heads = rel_bias.shape
    n_groups, pool_g, _ = pool_w.shape
    attn_w = n_heads * head_dim
    pool_width = n_groups * pool_g
    kv_w = (w_in.shape[1] - attn_w - pool_width) // 2
    n_kv = kv_w // head_dim
    group = n_heads // n_kv
    rows = TILE_ROWS
    n_seq = MIX_SEQS
    batch = n_tok // seq
    assert n_groups == len(POOL_WINDOWS) and pool_g == V7X_LANES
    assert 2 * head_dim == V7X_LANES and kv_w == V7X_LANES and group % 2 == 0
    assert seq % rows == 0 and rows % WINDOW == 0 and w_out.shape[0] == attn_w + pool_width
    assert batch % n_seq == 0

    q_gain = jnp.tile(q_norm, n_heads) * (LOG2E / math.sqrt(head_dim))

    row_spec = pl.BlockSpec((None, n_seq, rows, d), lambda b, j: (b, 0, j, 0))
    smem = pl.BlockSpec(memory_space=pltpu.SMEM)
    out = pl.pallas_call(
        _mixer_kernel,
        name="mixer",
        grid=(batch // n_seq, seq // rows),
        in_specs=[
            _const_spec((WINDOW, 2 * WINDOW)), smem, smem,
            row_spec, _const_spec((1, d)), _const_spec(w_in.shape),
            _const_spec((1, attn_w)), _const_spec((1, kv_w)),
            _const_spec(pool_w.shape), _const_spec((1, pool_width)), _const_spec(w_out.shape),
        ],
        out_specs=row_spec,
        out_shape=jax.ShapeDtypeStruct((batch // n_seq, n_seq, seq, d), F32),
        scratch_shapes=[
            pltpu.VMEM((n_heads, WINDOW, 2 * WINDOW), F32),
            pltpu.VMEM((n_heads, WINDOW, 2 * WINDOW), F32),
            pltpu.VMEM((n_seq, n_kv, rows // WINDOW, group * WINDOW, V7X_LANES), BF16),
            pltpu.VMEM((n_seq, n_kv, WINDOW + rows, V7X_LANES), BF16),
            pltpu.VMEM((n_seq, n_kv, WINDOW + rows, 2 * V7X_LANES), BF16),
            pltpu.VMEM((n_seq, MAX_POOL + rows, pool_width), F32),
            pltpu.VMEM((n_seq, rows, attn_w + pool_width), BF16),
        ],
        compiler_params=pltpu.CompilerParams(
            dimension_semantics=("arbitrary", "arbitrary"), vmem_limit_bytes=MIX_VMEM_LIMIT),
    )(_t5_bucket_table(n_buckets), rel_bias * LOG2E, sinks * LOG2E,
      x2.reshape(batch // n_seq, n_seq, seq, d), gain.reshape(1, d), w_in.astype(BF16),
      q_gain.reshape(1, attn_w), jnp.tile(k_norm, n_kv).reshape(1, kv_w),
      pool_w.astype(BF16), pool_scale.reshape(1, pool_width), w_out.astype(BF16))
    return out.reshape(n_tok, d)


def kernel(x, p, ffn1_norm, ffn1_w_gu, ffn1_w_down, mix_norm, w_in, q_norm, k_norm, rel_bias,
           sinks, pool_w, pool_scale, w_out, ffn2_norm, ffn2_w_gu, ffn2_w_down, ple_norm,
           ple_w_gate, ple_b_gate, ple_w_proj, ple_post_norm):
    batch, seq, d = x.shape
    x2 = x.reshape(batch * seq, d)
    for i in range(p.shape[0]):
        x2 = _ffn(x2, ffn1_norm[i], ffn1_w_gu[i], ffn1_w_down[i])
        x2 = _mixer(x2, seq, mix_norm[i], w_in[i], q_norm[i], k_norm[i], rel_bias, sinks[i],
                    pool_w[i], pool_scale[i], w_out[i])
        x2 = _ffn(x2, ffn2_norm[i], ffn2_w_gu[i], ffn2_w_down[i],
                  ple=(p[i].reshape(batch * seq, -1), ple_norm[i], ple_w_gate[i], ple_b_gate[i],
                       ple_w_proj[i], ple_post_norm[i]))
    return x2.reshape(batch, seq, d)
```

```python
import math

import jax
import jax.numpy as jnp
from jax import lax
from jax.experimental import pallas as pl
from jax.experimental.pallas import tpu as pltpu

F32 = jnp.float32
BF16 = jnp.bfloat16

WINDOW = 128
MAX_DISTANCE = 128
POOL_WINDOWS = (2, 4, 8, 16)
MACARON_WEIGHT = 0.5
EPS = 1e-6
NEG_INF = -1e30
LOG2E = math.log2(math.e)

V7X_LANES = 128
V7X_MXU_DIM = 256
F32_SUBLANES = 8
BF16_SUBLANES = 16

TILE_ROWS = 512
FFN_ROWS = 1024
MIX_SEQS = 2
FFN_CHUNK = V7X_MXU_DIM
MAX_POOL = max(POOL_WINDOWS)
FFN_VMEM_LIMIT = 56 * 1024 * 1024
MIX_VMEM_LIMIT = 48 * 1024 * 1024


def _rms(x, gain):
    ms = jnp.mean(x * x, axis=-1, keepdims=True)
    return x * lax.rsqrt(ms + EPS) * gain


def _silu_of_half(h):
    return h + h * jnp.tanh(h)


def _const_spec(shape):
    zeros = (0,) * len(shape)
    return pl.BlockSpec(shape, lambda *_: zeros, pipeline_mode=pl.Buffered(1))


def _ffn_half_step(x_ref, gain_ref, wgu_ref, wd_ref, h_ref, acc_ref):
    h_ref[...] = _rms(x_ref[...], gain_ref[...]).astype(BF16)
    d_ff = wd_ref.shape[0]
    n_chunks = d_ff // FFN_CHUNK
    for c in range(n_chunks):
        cols = slice(c * FFN_CHUNK, (c + 1) * FFN_CHUNK)
        ucols = slice(d_ff + c * FFN_CHUNK, d_ff + (c + 1) * FFN_CHUNK)
        half_g = jnp.dot(h_ref[...], wgu_ref[:, cols], preferred_element_type=F32)
        u = jnp.dot(h_ref[...], wgu_ref[:, ucols], preferred_element_type=F32)
        act = _silu_of_half(half_g) * u
        down = jnp.dot(act.astype(BF16), wd_ref[cols, :], preferred_element_type=F32)
        if c == 0:
            acc_ref[...] = down
        elif c < n_chunks - 1:
            acc_ref[...] += down
    return x_ref[...] + MACARON_WEIGHT * (acc_ref[...] + down)


def _ffn_kernel(x_ref, gain_ref, wgu_ref, wd_ref, o_ref, h_ref, acc_ref):
    o_ref[...] = _ffn_half_step(x_ref, gain_ref, wgu_ref, wd_ref, h_ref, acc_ref)


def _ffn_ple_kernel(x_ref, gain_ref, wgu_ref, wd_ref, p_ref, pgain_ref, wgate_ref, bgate_ref,
                    wproj_ref, post_ref, o_ref, h_ref, acc_ref):
    e = jnp.dot(p_ref[...].astype(BF16), wproj_ref[...], preferred_element_type=F32)
    x1 = _ffn_half_step(x_ref, gain_ref, wgu_ref, wd_ref, h_ref, acc_ref)
    half_e = 0.5 * _rms(e, post_ref[...])
    base = x1 + half_e
    hp = _rms(x1, pgain_ref[...]).astype(BF16)
    part = 2 * V7X_MXU_DIM
    for c0 in range(0, x1.shape[1], part):
        cols = slice(c0, c0 + part)
        half_z = jnp.dot(hp, wgate_ref[:, cols], preferred_element_type=F32) + bgate_ref[:, cols]
        o_ref[:, cols] = base[:, cols] + half_e[:, cols] * jnp.tanh(half_z)


def _ffn(x2, gain, w_gu, w_down, ple=None):
    n_tok, d = x2.shape
    d_ff = w_down.shape[0]
    assert d_ff % FFN_CHUNK == 0 and n_tok % FFN_ROWS == 0 and w_gu.shape == (d, 2 * d_ff)
    row_spec = pl.BlockSpec((FFN_ROWS, d), lambda s: (s, 0))
    in_specs = [row_spec, _const_spec((1, d)), _const_spec(w_gu.shape), _const_spec(w_down.shape)]
    gate_half = jnp.where(jnp.arange(2 * d_ff) < d_ff, 0.5, 1.0).astype(F32)
    args = [x2, gain.reshape(1, d), (w_gu * gate_half).astype(BF16), w_down.astype(BF16)]
    if ple is not None:
        p2, pgain, w_gate, b_gate, w_proj, post_gain = ple
        in_specs += [pl.BlockSpec((FFN_ROWS, p2.shape[1]), lambda s: (s, 0)), _const_spec((1, d)),
                     _const_spec(w_gate.shape), _const_spec((1, d)), _const_spec(w_proj.shape),
                     _const_spec((1, d))]
        args += [p2, pgain.reshape(1, d), (0.5 * w_gate).astype(BF16), (0.5 * b_gate).reshape(1, d),
                 w_proj.astype(BF16), post_gain.reshape(1, d)]
    return pl.pallas_call(
        _ffn_kernel if ple is None else _ffn_ple_kernel,
        name="ffn" if ple is None else "ffn_ple",
        grid=(n_tok // FFN_ROWS,),
        in_specs=in_specs,
        out_specs=row_spec,
        out_shape=jax.ShapeDtypeStruct((n_tok, d), F32),
        scratch_shapes=[pltpu.VMEM((FFN_ROWS, d), BF16), pltpu.VMEM((FFN_ROWS, d), F32)],
        compiler_params=pltpu.CompilerParams(
            dimension_semantics=("arbitrary",), vmem_limit_bytes=FFN_VMEM_LIMIT),
    )(*args)


def _head_rms_cols(t, low, gain):
    heads_per_col = 2
    cols = []
    for c in range(t.shape[1] // V7X_LANES):
        sl = slice(c * V7X_LANES, (c + 1) * V7X_LANES)
        tc = t[:, sl]
        sq = tc * tc
        ss_low = jnp.sum(jnp.where(low, sq, 0.0), axis=-1, keepdims=True)
        ss_high = jnp.sum(jnp.where(low, 0.0, sq), axis=-1, keepdims=True)
        ms = jnp.where(low, ss_low, ss_high) * (heads_per_col / V7X_LANES)
        cols.append(tc * lax.rsqrt(ms + EPS) * gain[:, sl])
    return cols


def _dup_half(t, lane, half):
    r = pltpu.roll(t, half, axis=1)
    low = lane < half
    return jnp.where(low, t, r), jnp.where(low, r, t)


def _build_score_tables(bucket_ref, relb_ref, sinks_ref, tbl_ref, fill_ref):
    n_buckets, n_heads = relb_ref.shape
    step = F32_SUBLANES

    def row_group(r, carry):
        r0 = pl.multiple_of(r * step, step)
        bidx = bucket_ref[pl.ds(r0, step), :]
        slot0 = lax.broadcasted_iota(jnp.int32, bidx.shape, 1) == 0
        accs = [jnp.zeros(bidx.shape, F32) for _ in range(n_heads)]
        for b in range(n_buckets):
            hit = bidx == b
            accs = [jnp.where(hit, relb_ref[b, h], accs[h]) for h in range(n_heads)]
        for h in range(n_heads):
            tbl_ref[h, pl.ds(r0, step), :] = accs[h]
            fill_ref[h, pl.ds(r0, step), :] = jnp.where(slot0, sinks_ref[h], NEG_INF)
        return carry

    lax.fori_loop(0, WINDOW // step, row_group, 0)


def _mixer_kernel(bucket_ref, relb_ref, sinks_ref, x_ref, gain_ref, win_ref, qg_ref, kg_ref,
                  poolw_ref, pscale_ref, wout_ref, o_ref,
                  tbl_ref, fill_ref, qs_ref, kctx_ref, vctx_ref, uext_ref, cat_ref):
    blk = WINDOW
    n_seq, rows, _ = x_ref.shape
    n_blk = rows // blk
    n_heads = tbl_ref.shape[0]
    n_kv = kctx_ref.shape[1]
    group = n_heads // n_kv
    head_dim = V7X_LANES // 2
    attn_w = n_heads * head_dim
    kv_w = n_kv * head_dim
    pool_w = cat_ref.shape[-1] - attn_w
    pool_g = pool_w // len(POOL_WINDOWS)
    j = pl.program_id(1)

    @pl.when((pl.program_id(0) == 0) & (j == 0))
    def _first_step():
        _build_score_tables(bucket_ref, relb_ref, sinks_ref, tbl_ref, fill_ref)

    @pl.when(j == 0)
    def _reset_carry():
        kctx_ref[:, :, 0:blk, :] = jnp.zeros((n_seq, n_kv, blk, V7X_LANES), BF16)
        vctx_ref[:, :, 0:blk, :] = jnp.zeros((n_seq, n_kv, blk, 2 * V7X_LANES), BF16)
        uext_ref[:, 0:MAX_POOL, :] = jnp.zeros((n_seq, MAX_POOL, pool_w), F32)

    lane = lax.broadcasted_iota(jnp.int32, (rows, V7X_LANES), 1)
    low = lane < head_dim
    ones = jnp.ones((rows, V7X_LANES), BF16)
    qi = lax.broadcasted_iota(jnp.int32, (blk, 2 * blk), 0)
    kj = lax.broadcasted_iota(jnp.int32, (blk, 2 * blk), 1)
    dist = qi + blk - kj
    band = (dist >= 0) & (dist < WINDOW)
    low_blk = lax.broadcasted_iota(jnp.int32, (blk, V7X_LANES), 1) < head_dim
    t_pos = j * rows + lax.broadcasted_iota(jnp.int32, (rows, 1), 0)
    top_row = lax.broadcasted_iota(jnp.int32, (BF16_SUBLANES, 2 * V7X_LANES), 0) == 0
    sink_row = (lax.broadcasted_iota(jnp.int32, (BF16_SUBLANES, 2 * V7X_LANES), 1)
                >= V7X_LANES).astype(F32)

    projs = []
    for s in range(n_seq):
        h = _rms(x_ref[s], gain_ref[...]).astype(BF16)
        projs.append(jnp.dot(h, win_ref[...], preferred_element_type=F32))

    for s, proj in enumerate(projs):
        q = proj[:, 0:attn_w]
        k = proj[:, attn_w:attn_w + kv_w]
        v = proj[:, attn_w + kv_w:attn_w + 2 * kv_w]
        uext_ref[s, MAX_POOL:MAX_POOL + rows, :] = proj[:, attn_w + 2 * kv_w:]
        for c, qc in enumerate(_head_rms_cols(q, low, qg_ref[...])):
            q_lo = jnp.where(low, qc, 0.0).astype(BF16)
            q_hi = jnp.where(low, 0.0, qc).astype(BF16)
            kvh, slot = divmod(2 * c, group)
            for b in range(n_blk):
                qs_ref[s, kvh, b, slot * blk:(slot + 1) * blk, :] = q_lo[b * blk:(b + 1) * blk]
                qs_ref[s, kvh, b, (slot + 1) * blk:(slot + 2) * blk, :] = q_hi[b * blk:(b + 1) * blk]
        (kn,) = _head_rms_cols(k, low, kg_ref[...])
        kv_pairs = zip(_dup_half(kn, lane, head_dim), _dup_half(v, lane, head_dim))
        for kvh, (kd, vd) in enumerate(kv_pairs):
            kctx_ref[s, kvh, blk:blk + rows, :] = kd.astype(BF16)
            vctx_ref[s, kvh, blk:blk + rows, 0:V7X_LANES] = vd.astype(BF16)
            vctx_ref[s, kvh, blk:blk + rows, V7X_LANES:] = ones

    def scores(s, b, kvh):
        kc = kctx_ref[s, kvh, b * blk:(b + 2) * blk, :]
        return lax.dot_general(qs_ref[s, kvh, b], kc, (((1,), (1,)), ((), ())),
                               preferred_element_type=F32)

    def attend(sc, s, b, kvh):
        r0 = b * blk
        valid = band if b > 0 else band & ((kj >= blk) | (j > 0))
        es = []
        for i in range(group):
            head = kvh * group + i
            si = jnp.where(valid, sc[i * blk:(i + 1) * blk] + tbl_ref[head], fill_ref[head])
            m = jnp.max(si, axis=-1, keepdims=True)
            es.append(jnp.exp2(si - m).astype(BF16))
        top = vctx_ref[s, kvh, r0:r0 + BF16_SUBLANES, :].astype(F32)
        top = jnp.where(top_row, sink_row, top).astype(BF16)
        vc = jnp.concatenate([top, vctx_ref[s, kvh, r0 + BF16_SUBLANES:r0 + 2 * blk, :]], axis=0)
        pv = jnp.dot(jnp.concatenate(es, axis=0), vc, preferred_element_type=F32)
        outs = [pv[i * blk:(i + 1) * blk, 0:V7X_LANES] / pv[i * blk:(i + 1) * blk, V7X_LANES:]
                for i in range(group)]
        for pair in range(group // 2):
            col = (kvh * group + 2 * pair) * head_dim
            both = jnp.where(low_blk, outs[2 * pair], outs[2 * pair + 1])
            cat_ref[s, r0:r0 + blk, col:col + V7X_LANES] = both.astype(BF16)

    items = [(s, b, kvh) for b in range(n_blk) for kvh in range(n_kv) for s in range(n_seq)]
    sc_next = scores(*items[0])
    for n, item in enumerate(items):
        sc = sc_next
        if n + 1 < len(items):
            sc_next = scores(*items[n + 1])
        attend(sc, *item)

    for g, w in enumerate(POOL_WINDOWS):
        cols = slice(g * pool_g, (g + 1) * pool_g)
        cnt = jnp.minimum(t_pos + 1, w).astype(F32)
        for s in range(n_seq):
            total = uext_ref[s, :, cols]
            shift = 1
            while shift < w:
                total = total + pltpu.roll(total, shift, axis=0)
                shift *= 2
            d = total[MAX_POOL:] * (1.0 / cnt) - uext_ref[s, MAX_POOL:, cols]
            y = jnp.dot(d.astype(BF16), poolw_ref[g], preferred_element_type=F32)
            cat_ref[s, :, attn_w + g * pool_g:attn_w + (g + 1) * pool_g] = (
                y * pscale_ref[:, cols]).astype(BF16)

    for s in range(n_seq):
        o_ref[s] = x_ref[s] + jnp.dot(cat_ref[s], wout_ref[...], preferred_element_type=F32)
    kctx_ref[:, :, 0:blk, :] = kctx_ref[:, :, rows:rows + blk, :]
    vctx_ref[:, :, 0:blk, :] = vctx_ref[:, :, rows:rows + blk, :]
    uext_ref[:, 0:MAX_POOL, :] = uext_ref[:, rows:rows + MAX_POOL, :]


def _t5_bucket_table(n_buckets):
    qi = jnp.arange(WINDOW)[:, None]
    kj = jnp.arange(2 * WINDOW)[None, :]
    n = jnp.maximum(qi + WINDOW - kj, 0)
    max_exact = n_buckets // 2
    nf = jnp.maximum(n, 1).astype(F32)
    large = max_exact + (jnp.log(nf / max_exact) / math.log(MAX_DISTANCE / max_exact)
                         * (n_buckets - max_exact)).astype(jnp.int32)
    large = jnp.minimum(large, n_buckets - 1)
    return jnp.where(n < max_exact, n, large).astype(jnp.int32)


def _mixer(x2, seq, gain, w_in, q_norm, k_norm, rel_bias, sinks, pool_w, pool_scale, w_out):
    n_tok, d = x2.shape
    head_dim = q_norm.shape[0]
    n_buckets, n_heads = rel_bias.shape
    n_groups, pool_g, _ = pool_w.shape
    attn_w = n_heads * head_dim
    pool_width = n_groups * pool_g
    kv_w = (w_in.shape[1] - attn_w - pool_width) // 2
    n_kv = kv_w // head_dim
    group = n_heads // n_kv
    rows = TILE_ROWS
    n_seq = MIX_SEQS
    batch = n_tok // seq
    assert n_groups == len(POOL_WINDOWS) and pool_g == V7X_LANES
    assert 2 * head_dim == V7X_LANES and kv_w == V7X_LANES and group % 2 == 0
    assert seq % rows == 0 and rows % WINDOW == 0 and w_out.shape[0] == attn_w + pool_width
    assert batch % n_seq == 0

    q_gain = jnp.tile(q_norm, n_heads) * (LOG2E / math.sqrt(head_dim))

    row_spec = pl.BlockSpec((None, n_seq, rows, d), lambda b, j: (b, 0, j, 0))
    smem = pl.BlockSpec(memory_space=pltpu.SMEM)
    out = pl.pallas_call(
        _mixer_kernel,
        name="mixer",
        grid=(batch // n_seq, seq // rows),
        in_specs=[
            _const_spec((WINDOW, 2 * WINDOW)), smem, smem,
            row_spec, _const_spec((1, d)), _const_spec(w_in.shape),
            _const_spec((1, attn_w)), _const_spec((1, kv_w)),
            _const_spec(pool_w.shape), _const_spec((1, pool_width)), _const_spec(w_out.shape),
        ],
        out_specs=row_spec,
        out_shape=jax.ShapeDtypeStruct((batch // n_seq, n_seq, seq, d), F32),
        scratch_shapes=[
            pltpu.VMEM((n_heads, WINDOW, 2 * WINDOW), F32),
            pltpu.VMEM((n_heads, WINDOW, 2 * WINDOW), F32),
            pltpu.VMEM((n_seq, n_kv, rows // WINDOW, group * WINDOW, V7X_LANES), BF16),
            pltpu.VMEM((n_seq, n_kv, WINDOW + rows, V7X_LANES), BF16),
            pltpu.VMEM((n_seq, n_kv, WINDOW + rows, 2 * V7X_LANES), BF16),
            pltpu.VMEM((n_seq, MAX_POOL + rows, pool_width), F32),
            pltpu.VMEM((n_seq, rows, attn_w + pool_width), BF16),
        ],
        compiler_params=pltpu.CompilerParams(
            dimension_semantics=("arbitrary", "arbitrary"), vmem_limit_bytes=MIX_VMEM_LIMIT),
    )(_t5_bucket_table(n_buckets), rel_bias * LOG2E, sinks * LOG2E,
      x2.reshape(batch // n_seq, n_seq, seq, d), gain.reshape(1, d), w_in.astype(BF16),
      q_gain.reshape(1, attn_w), jnp.tile(k_norm, n_kv).reshape(1, kv_w),
      pool_w.astype(BF16), pool_scale.reshape(1, pool_width), w_out.astype(BF16))
    return out.reshape(n_tok, d)


def kernel(x, p, ffn1_norm, ffn1_w_gu, ffn1_w_down, mix_norm, w_in, q_norm, k_norm, rel_bias,
           sinks, pool_w, pool_scale, w_out, ffn2_norm, ffn2_w_gu, ffn2_w_down, ple_norm,
           ple_w_gate, ple_b_gate, ple_w_proj, ple_post_norm):
    batch, seq, d = x.shape
    x2 = x.reshape(batch * seq, d)
    for i in range(p.shape[0]):
        x2 = _ffn(x2, ffn1_norm[i], ffn1_w_gu[i], ffn1_w_down[i])
        x2 = _mixer(x2, seq, mix_norm[i], w_in[i], q_norm[i], k_norm[i], rel_bias, sinks[i],
                    pool_w[i], pool_scale[i], w_out[i])
        x2 = _ffn(x2, ffn2_norm[i], ffn2_w_gu[i], ffn2_w_down[i],
                  ple=(p[i].reshape(batch * seq, -1), ple_norm[i], ple_w_gate[i], ple_b_gate[i],
                       ple_w_proj[i], ple_post_norm[i]))
    return x2.reshape(batch, seq, d)
```

```python
import math

import jax
import jax.numpy as jnp
from jax import lax
from jax.experimental import pallas as pl
from jax.experimental.pallas import tpu as pltpu

F32 = jnp.float32
BF16 = jnp.bfloat16

WINDOW = 128
MAX_DISTANCE = 128
POOL_WINDOWS = (2, 4, 8, 16)
MACARON_WEIGHT = 0.5
EPS = 1e-6
NEG_INF = -1e30
LOG2E = math.log2(math.e)

V7X_LANES = 128
V7X_MXU_DIM = 256
F32_SUBLANES = 8
BF16_SUBLANES = 16

TILE_ROWS = 512
FFN_ROWS = 1024
MIX_SEQS = 2
FFN_CHUNK = V7X_MXU_DIM
MAX_POOL = max(POOL_WINDOWS)
FFN_VMEM_LIMIT = 56 * 1024 * 1024
MIX_VMEM_LIMIT = 48 * 1024 * 1024


def _rms(x, gain):
    ms = jnp.mean(x * x, axis=-1, keepdims=True)
    return x * lax.rsqrt(ms + EPS) * gain


def _silu_of_half(h):
    return h + h * jnp.tanh(h)


def _const_spec(shape):
    zeros = (0,) * len(shape)
    return pl.BlockSpec(shape, lambda *_: zeros, pipeline_mode=pl.Buffered(1))


def _ffn_half_step(x_ref, gain_ref, wgu_ref, wd_ref, h_ref, acc_ref):
    h_ref[...] = _rms(x_ref[...], gain_ref[...]).astype(BF16)
    d_ff = wd_ref.shape[0]
    n_chunks = d_ff // FFN_CHUNK
    for c in range(n_chunks):
        cols = slice(c * FFN_CHUNK, (c + 1) * FFN_CHUNK)
        ucols = slice(d_ff + c * FFN_CHUNK, d_ff + (c + 1) * FFN_CHUNK)
        half_g = jnp.dot(h_ref[...], wgu_ref[:, cols], preferred_element_type=F32)
        u = jnp.dot(h_ref[...], wgu_ref[:, ucols], preferred_element_type=F32)
        act = _silu_of_half(half_g) * u
        down = jnp.dot(act.astype(BF16), wd_ref[cols, :], preferred_element_type=F32)
        if c == 0:
            acc_ref[...] = down
        elif c < n_chunks - 1:
            acc_ref[...] += down
    return x_ref[...] + MACARON_WEIGHT * (acc_ref[...] + down)


def _stage_weight(w_hbm, w_ref, stage_ref, sem_ref, axis, scale_below=None):
    size = stage_ref.shape[1 + axis]
    n = w_ref.shape[axis] // size

    def where(i):
        part = slice(i * size, (i + 1) * size)
        return (slice(None), part) if axis == 1 else (part, slice(None))

    copies = [pltpu.make_async_copy(w_hbm.at[where(i)], stage_ref.at[i % 2], sem_ref.at[i % 2])
              for i in range(n)]
    copies[0].start()
    for i in range(n):
        if i + 1 < n:
            copies[i + 1].start()
        copies[i].wait()
        chunk = stage_ref[i % 2]
        if scale_below is not None and i * size < scale_below:
            col = i * size + lax.broadcasted_iota(jnp.int32, (1, size), 1)
            chunk = chunk * jnp.where(col < scale_below, 0.5, 1.0)
        w_ref[where(i)] = chunk.astype(BF16)


def _stage_ffn_weights(wgu_hbm, wd_hbm, wgu_ref, wd_ref, stage_gu, stage_d, sem_gu, sem_d):
    @pl.when(pl.program_id(0) == 0)
    def _():
        _stage_weight(wgu_hbm, wgu_ref, stage_gu, sem_gu, axis=1, scale_below=wd_ref.shape[0])
        _stage_weight(wd_hbm, wd_ref, stage_d, sem_d, axis=0)


def _ffn_kernel(x_ref, gain_ref, wgu_hbm, wd_hbm, o_ref, h_ref, acc_ref, wgu_ref, wd_ref,
                stage_gu, stage_d, sem_gu, sem_d):
    _stage_ffn_weights(wgu_hbm, wd_hbm, wgu_ref, wd_ref, stage_gu, stage_d, sem_gu, sem_d)
    o_ref[...] = _ffn_half_step(x_ref, gain_ref, wgu_ref, wd_ref, h_ref, acc_ref)


def _ffn_ple_kernel(x_ref, gain_ref, wgu_hbm, wd_hbm, p_ref, pgain_ref, wgate_ref, bgate_ref,
                    wproj_ref, post_ref, o_ref, h_ref, acc_ref, wgu_ref, wd_ref,
                    stage_gu, stage_d, sem_gu, sem_d):
    _stage_ffn_weights(wgu_hbm, wd_hbm, wgu_ref, wd_ref, stage_gu, stage_d, sem_gu, sem_d)
    e = jnp.dot(p_ref[...].astype(BF16), wproj_ref[...], preferred_element_type=F32)
    x1 = _ffn_half_step(x_ref, gain_ref, wgu_ref, wd_ref, h_ref, acc_ref)
    half_e = 0.5 * _rms(e, post_ref[...])
    base = x1 + half_e
    hp = _rms(x1, pgain_ref[...]).astype(BF16)
    part = 2 * V7X_MXU_DIM
    for c0 in range(0, x1.shape[1], part):
        cols = slice(c0, c0 + part)
        half_z = jnp.dot(hp, wgate_ref[:, cols], preferred_element_type=F32) + bgate_ref[:, cols]
        o_ref[:, cols] = base[:, cols] + half_e[:, cols] * jnp.tanh(half_z)


def _ffn(x2, gain, w_gu, w_down, ple=None):
    n_tok, d = x2.shape
    d_ff = w_down.shape[0]
    assert d_ff % FFN_CHUNK == 0 and n_tok % FFN_ROWS == 0 and w_gu.shape == (d, 2 * d_ff)
    row_spec = pl.BlockSpec((FFN_ROWS, d), lambda s: (s, 0))
    in_hbm = pl.BlockSpec(memory_space=pl.ANY)
    in_specs = [row_spec, _const_spec((1, d)), in_hbm, in_hbm]
    args = [x2, gain.reshape(1, d), w_gu, w_down]
    gu_stage_cols = 2 * FFN_CHUNK
    assert (2 * d_ff) % gu_stage_cols == 0
    if ple is not None:
        p2, pgain, w_gate, b_gate, w_proj, post_gain = ple
        in_specs += [pl.BlockSpec((FFN_ROWS, p2.shape[1]), lambda s: (s, 0)), _const_spec((1, d)),
                     _const_spec(w_gate.shape), _const_spec((1, d)), _const_spec(w_proj.shape),
                     _const_spec((1, d))]
        args += [p2, pgain.reshape(1, d), (0.5 * w_gate).astype(BF16), (0.5 * b_gate).reshape(1, d),
                 w_proj.astype(BF16), post_gain.reshape(1, d)]
    return pl.pallas_call(
        _ffn_kernel if ple is None else _ffn_ple_kernel,
        name="ffn" if ple is None else "ffn_ple",
        grid=(n_tok // FFN_ROWS,),
        in_specs=in_specs,
        out_specs=row_spec,
        out_shape=jax.ShapeDtypeStruct((n_tok, d), F32),
        scratch_shapes=[pltpu.VMEM((FFN_ROWS, d), BF16), pltpu.VMEM((FFN_ROWS, d), F32),
                        pltpu.VMEM((d, 2 * d_ff), BF16), pltpu.VMEM((d_ff, d), BF16),
                        pltpu.VMEM((2, d, gu_stage_cols), F32), pltpu.VMEM((2, FFN_CHUNK, d), F32),
                        pltpu.SemaphoreType.DMA((2,)), pltpu.SemaphoreType.DMA((2,))],
        compiler_params=pltpu.CompilerParams(
            dimension_semantics=("arbitrary",), vmem_limit_bytes=FFN_VMEM_LIMIT),
    )(*args)


def _head_rms_cols(t, low, gain):
    heads_per_col = 2
    cols = []
    for c in range(t.shape[1] // V7X_LANES):
        sl = slice(c * V7X_LANES, (c + 1) * V7X_LANES)
        tc = t[:, sl]
        sq = tc * tc
        ss_low = jnp.sum(jnp.where(low, sq, 0.0), axis=-1, keepdims=True)
        ss_high = jnp.sum(jnp.where(low, 0.0, sq), axis=-1, keepdims=True)
        ms = jnp.where(low, ss_low, ss_high) * (heads_per_col / V7X_LANES)
        cols.append(tc * lax.rsqrt(ms + EPS) * gain[:, sl])
    return cols


def _dup_half(t, lane, half):
    r = pltpu.roll(t, half, axis=1)
    low = lane < half
    return jnp.where(low, t, r), jnp.where(low, r, t)


def _build_score_tables(bucket_ref, relb_ref, sinks_ref, tbl_ref, fill_ref):
    n_buckets, n_heads = relb_ref.shape
    step = F32_SUBLANES

    def row_group(r, carry):
        r0 = pl.multiple_of(r * step, step)
        bidx = bucket_ref[pl.ds(r0, step), :]
        slot0 = lax.broadcasted_iota(jnp.int32, bidx.shape, 1) == 0
        accs = [jnp.zeros(bidx.shape, F32) for _ in range(n_heads)]
        for b in range(n_buckets):
            hit = bidx == b
            accs = [jnp.where(hit, relb_ref[b, h], accs[h]) for h in range(n_heads)]
        for h in range(n_heads):
            tbl_ref[h, pl.ds(r0, step), :] = accs[h]
            fill_ref[h, pl.ds(r0, step), :] = jnp.where(slot0, sinks_ref[h], NEG_INF)
        return carry

    lax.fori_loop(0, WINDOW // step, row_group, 0)


def _mixer_kernel(bucket_ref, relb_ref, sinks_ref, x_ref, gain_ref, win_ref, qg_ref, kg_ref,
                  poolw_ref, pscale_ref, wout_ref, o_ref,
                  tbl_ref, fill_ref, qs_ref, kctx_ref, vctx_ref, uext_ref, cat_ref):
    blk = WINDOW
    n_seq, rows, _ = x_ref.shape
    n_blk = rows // blk
    n_heads = tbl_ref.shape[0]
    n_kv = kctx_ref.shape[1]
    group = n_heads // n_kv
    head_dim = V7X_LANES // 2
    attn_w = n_heads * head_dim
    kv_w = n_kv * head_dim
    pool_w = cat_ref.shape[-1] - attn_w
    pool_g = pool_w // len(POOL_WINDOWS)
    j = pl.program_id(1)

    @pl.when((pl.program_id(0) == 0) & (j == 0))
    def _first_step():
        _build_score_tables(bucket_ref, relb_ref, sinks_ref, tbl_ref, fill_ref)

    @pl.when(j == 0)
    def _reset_carry():
        kctx_ref[:, :, 0:blk, :] = jnp.zeros((n_seq, n_kv, blk, V7X_LANES), BF16)
        vctx_ref[:, :, 0:blk, :] = jnp.zeros((n_seq, n_kv, blk, 2 * V7X_LANES), BF16)
        uext_ref[:, 0:MAX_POOL, :] = jnp.zeros((n_seq, MAX_POOL, pool_w), F32)

    lane = lax.broadcasted_iota(jnp.int32, (rows, V7X_LANES), 1)
    low = lane < head_dim
    ones = jnp.ones((rows, V7X_LANES), BF16)
    qi = lax.broadcasted_iota(jnp.int32, (blk, 2 * blk), 0)
    kj = lax.broadcasted_iota(jnp.int32, (blk, 2 * blk), 1)
    dist = qi + blk - kj
    band = (dist >= 0) & (dist < WINDOW)
    low_blk = lax.broadcasted_iota(jnp.int32, (blk, V7X_LANES), 1) < head_dim
    t_pos = j * rows + lax.broadcasted_iota(jnp.int32, (rows, 1), 0)
    top_row = lax.broadcasted_iota(jnp.int32, (BF16_SUBLANES, 2 * V7X_LANES), 0) == 0
    sink_row = (lax.broadcasted_iota(jnp.int32, (BF16_SUBLANES, 2 * V7X_LANES), 1)
                >= V7X_LANES).astype(F32)

    projs = []
    for s in range(n_seq):
        h = _rms(x_ref[s], gain_ref[...]).astype(BF16)
        projs.append(jnp.dot(h, win_ref[...], preferred_element_type=F32))

    for s, proj in enumerate(projs):
        q = proj[:, 0:attn_w]
        k = proj[:, attn_w:attn_w + kv_w]
        v = proj[:, attn_w + kv_w:attn_w + 2 * kv_w]
        uext_ref[s, MAX_POOL:MAX_POOL + rows, :] = proj[:, attn_w + 2 * kv_w:]
        for c, qc in enumerate(_head_rms_cols(q, low, qg_ref[...])):
            q_lo = jnp.where(low, qc, 0.0).astype(BF16)
            q_hi = jnp.where(low, 0.0, qc).astype(BF16)
            kvh, slot = divmod(2 * c, group)
            for b in range(n_blk):
                qs_ref[s, kvh, b, slot * blk:(slot + 1) * blk, :] = q_lo[b * blk:(b + 1) * blk]
                qs_ref[s, kvh, b, (slot + 1) * blk:(slot + 2) * blk, :] = q_hi[b * blk:(b + 1) * blk]
        (kn,) = _head_rms_cols(k, low, kg_ref[...])
        kv_pairs = zip(_dup_half(kn, lane, head_dim), _dup_half(v, lane, head_dim))
        for kvh, (kd, vd) in enumerate(kv_pairs):
            kctx_ref[s, kvh, blk:blk + rows, :] = kd.astype(BF16)
            vctx_ref[s, kvh, blk:blk + rows, 0:V7X_LANES] = vd.astype(BF16)
            vctx_ref[s, kvh, blk:blk + rows, V7X_LANES:] = ones

    def scores(s, b, kvh):
        kc = kctx_ref[s, kvh, b * blk:(b + 2) * blk, :]
        return lax.dot_general(qs_ref[s, kvh, b], kc, (((1,), (1,)), ((), ())),
                               preferred_element_type=F32)

    def attend(sc, s, b, kvh):
        r0 = b * blk
        valid = band if b > 0 else band & ((kj >= blk) | (j > 0))
        es = []
        for i in range(group):
            head = kvh * group + i
            si = jnp.where(valid, sc[i * blk:(i + 1) * blk] + tbl_ref[head], fill_ref[head])
            m = jnp.max(si, axis=-1, keepdims=True)
            es.append(jnp.exp2(si - m).astype(BF16))
        top = vctx_ref[s, kvh, r0:r0 + BF16_SUBLANES, :].astype(F32)
        top = jnp.where(top_row, sink_row, top).astype(BF16)
        vc = jnp.concatenate([top, vctx_ref[s, kvh, r0 + BF16_SUBLANES:r0 + 2 * blk, :]], axis=0)
        pv = jnp.dot(jnp.concatenate(es, axis=0), vc, preferred_element_type=F32)
        outs = [pv[i * blk:(i + 1) * blk, 0:V7X_LANES] / pv[i * blk:(i + 1) * blk, V7X_LANES:]
                for i in range(group)]
        for pair in range(group // 2):
            col = (kvh * group + 2 * pair) * head_dim
            both = jnp.where(low_blk, outs[2 * pair], outs[2 * pair + 1])
            cat_ref[s, r0:r0 + blk, col:col + V7X_LANES] = both.astype(BF16)

    items = [(s, b, kvh) for b in range(n_blk) for kvh in range(n_kv) for s in range(n_seq)]
    sc_next = scores(*items[0])
    for n, item in enumerate(items):
        sc = sc_next
        if n + 1 < len(items):
            sc_next = scores(*items[n + 1])
        attend(sc, *item)

    for g, w in enumerate(POOL_WINDOWS):
        cols = slice(g * pool_g, (g + 1) * pool_g)
        cnt = jnp.minimum(t_pos + 1, w).astype(F32)
        for s in range(n_seq):
            total = uext_ref[s, :, cols]
            shift = 1
            while shift < w:
                total = total + pltpu.roll(total, shift, axis=0)
                shift *= 2
            d = total[MAX_POOL:] * (1.0 / cnt) - uext_ref[s, MAX_POOL:, cols]
            y = jnp.dot(d.astype(BF16), poolw_ref[g], preferred_element_type=F32)
            cat_ref[s, :, attn_w + g * pool_g:attn_w + (g + 1) * pool_g] = (
                y * pscale_ref[:, cols]).astype(BF16)

    for s in range(n_seq):
        o_ref[s] = x_ref[s] + jnp.dot(cat_ref[s], wout_ref[...], preferred_element_type=F32)
    kctx_ref[:, :, 0:blk, :] = kctx_ref[:, :, rows:rows + blk, :]
    vctx_ref[:, :, 0:blk, :] = vctx_ref[:, :, rows:rows + blk, :]
    uext_ref[:, 0:MAX_POOL, :] = uext_ref[:, rows:rows + MAX_POOL, :]


def _t5_bucket_table(n_buckets):
    qi = jnp.arange(WINDOW)[:, None]
    kj = jnp.arange(2 * WINDOW)[None, :]
    n = jnp.maximum(qi + WINDOW - kj, 0)
    max_exact = n_buckets // 2
    nf = jnp.maximum(n, 1).astype(F32)
    large = max_exact + (jnp.log(nf / max_exact) / math.log(MAX_DISTANCE / max_exact)
                         * (n_buckets - max_exact)).astype(jnp.int32)
    large = jnp.minimum(large, n_buckets - 1)
    return jnp.where(n < max_exact, n, large).astype(jnp.int32)


def _mixer(x2, seq, gain, w_in, q_norm, k_norm, rel_bias, sinks, pool_w, pool_scale, w_out):
    n_tok, d = x2.shape
    head_dim = q_norm.shape[0]
    n_buckets, n_heads = rel_bias.shape
    n_groups, pool_g, _ = pool_w.shape
    attn_w = n_heads * head_dim
    pool_width = n_groups * pool_g
    kv_w = (w_in.shape[1] - attn_w - pool_width) // 2
    n_kv = kv_w // head_dim
    group = n_heads // n_kv
    rows = TILE_ROWS
    n_seq = MIX_SEQS
    batch = n_tok // seq
    assert n_groups == len(POOL_WINDOWS) and pool_g == V7X_LANES
    assert 2 * head_dim == V7X_LANES and kv_w == V7X_LANES and group % 2 == 0
    assert seq % rows == 0 and rows % WINDOW == 0 and w_out.shape[0] == attn_w + pool_width
    assert batch % n_seq == 0

    q_gain = jnp.tile(q_norm, n_heads) * (LOG2E / math.sqrt(head_dim))

    row_spec = pl.BlockSpec((None, n_seq, rows, d), lambda b, j: (b, 0, j, 0))
    smem = pl.BlockSpec(memory_space=pltpu.SMEM)
    out = pl.pallas_call(
        _mixer_kernel,
        name="mixer",
        grid=(batch // n_seq, seq // rows),
        in_specs=[
            _const_spec((WINDOW, 2 * WINDOW)), smem, smem,
            row_spec, _const_spec((1, d)), _const_spec(w_in.shape),
            _const_spec((1, attn_w)), _const_spec((1, kv_w)),
            _const_spec(pool_w.shape), _const_spec((1, pool_width)), _const_spec(w_out.shape),
        ],
        out_specs=row_spec,
        out_shape=jax.ShapeDtypeStruct((batch // n_seq, n_seq, seq, d), F32),
        scratch_shapes=[
            pltpu.VMEM((n_heads, WINDOW, 2 * WINDOW), F32),
            pltpu.VMEM((n_heads, WINDOW, 2 * WINDOW), F32),
            pltpu.VMEM((n_seq, n_kv, rows // WINDOW, group * WINDOW, V7X_LANES), BF16),
            pltpu.VMEM((n_seq, n_kv, WINDOW + rows, V7X_LANES), BF16),
            pltpu.VMEM((n_seq, n_kv, WINDOW + rows, 2 * V7X_LANES), BF16),
            pltpu.VMEM((n_seq, MAX_POOL + rows, pool_width), F32),
            pltpu.VMEM((n_seq, rows, attn_w + pool_width), BF16),
        ],
        compiler_params=pltpu.CompilerParams(
            dimension_semantics=("arbitrary", "arbitrary"), vmem_limit_bytes=MIX_VMEM_LIMIT),
    )(_t5_bucket_table(n_buckets), rel_bias * LOG2E, sinks * LOG2E,
      x2.reshape(batch // n_seq, n_seq, seq, d), gain.reshape(1, d), w_in.astype(BF16),
      q_gain.reshape(1, attn_w), jnp.tile(k_norm, n_kv).reshape(1, kv_w),
      pool_w.astype(BF16), pool_scale.reshape(1, pool_width), w_out.astype(BF16))
    return out.reshape(n_tok, d)


def kernel(x, p, ffn1_norm, ffn1_w_gu, ffn1_w_down, mix_norm, w_in, q_norm, k_norm, rel_bias,
           sinks, pool_w, pool_scale, w_out, ffn2_norm, ffn2_w_gu, ffn2_w_down, ple_norm,
           ple_w_gate, ple_b_gate, ple_w_proj, ple_post_norm):
    batch, seq, d = x.shape
    x2 = x.reshape(batch * seq, d)
    for i in range(p.shape[0]):
        x2 = _ffn(x2, ffn1_norm[i], ffn1_w_gu[i], ffn1_w_down[i])
        x2 = _mixer(x2, seq, mix_norm[i], w_in[i], q_norm[i], k_norm[i], rel_bias, sinks[i],
                    pool_w[i], pool_scale[i], w_out[i])
        x2 = _ffn(x2, ffn2_norm[i], ffn2_w_gu[i], ffn2_w_down[i],
                  ple=(p[i].reshape(batch * seq, -1), ple_norm[i], ple_w_gate[i], ple_b_gate[i],
                       ple_w_proj[i], ple_post_norm[i]))
    return x2.reshape(batch, seq, d)
```

```python
import math

import jax
import jax.numpy as jnp
from jax import lax
from jax.experimental import pallas as pl
from jax.experimental.pallas import tpu as pltpu

F32 = jnp.float32
BF16 = jnp.bfloat16

WINDOW = 128
MAX_DISTANCE = 128
POOL_WINDOWS = (2, 4, 8, 16)
MACARON_WEIGHT = 0.5
EPS = 1e-6
NEG_INF = -1e30
LOG2E = math.log2(math.e)

V7X_LANES = 128
V7X_MXU_DIM = 256
F32_SUBLANES = 8
BF16_SUBLANES = 16

TILE_ROWS = 512
FFN_ROWS = 1024
MIX_SEQS = 2
FFN_CHUNK = V7X_MXU_DIM
MAX_POOL = max(POOL_WINDOWS)
FFN_VMEM_LIMIT = 56 * 1024 * 1024
MIX_VMEM_LIMIT = 48 * 1024 * 1024


def _rms(x, gain):
    ms = jnp.mean(x * x, axis=-1, keepdims=True)
    return x * lax.rsqrt(ms + EPS) * gain


def _silu_of_half(h):
    return h + h * jnp.tanh(h)


def _const_spec(shape):
    zeros = (0,) * len(shape)
    return pl.BlockSpec(shape, lambda *_: zeros, pipeline_mode=pl.Buffered(1))


def _ffn_half_step(x_ref, gain_ref, wgu_ref, wd_ref, h_ref, acc_ref):
    h_ref[...] = _rms(x_ref[...], gain_ref[...]).astype(BF16)
    d_ff = wd_ref.shape[0]
    n_chunks = d_ff // FFN_CHUNK
    for c in range(n_chunks):
        cols = slice(c * FFN_CHUNK, (c + 1) * FFN_CHUNK)
        ucols = slice(d_ff + c * FFN_CHUNK, d_ff + (c + 1) * FFN_CHUNK)
        half_g = jnp.dot(h_ref[...], wgu_ref[:, cols], preferred_element_type=F32)
        u = jnp.dot(h_ref[...], wgu_ref[:, ucols], preferred_element_type=F32)
        act = _silu_of_half(half_g) * u
        down = jnp.dot(act.astype(BF16), wd_ref[cols, :], preferred_element_type=F32)
        if c == 0:
            acc_ref[...] = down
        elif c < n_chunks - 1:
            acc_ref[...] += down
    return x_ref[...] + MACARON_WEIGHT * (acc_ref[...] + down)


def _stage_weight(w_hbm, w_ref, stage_ref, sem_ref, axis, priority, scale_below=None):
    size = stage_ref.shape[1 + axis]
    n = w_ref.shape[axis] // size

    def where(i):
        part = slice(i * size, (i + 1) * size)
        return (slice(None), part) if axis == 1 else (part, slice(None))

    copies = [pltpu.make_async_copy(w_hbm.at[where(i)], stage_ref.at[i % 2], sem_ref.at[i % 2])
              for i in range(n)]
    copies[0].start(priority)
    yield
    for i in range(n):
        if i + 1 < n:
            copies[i + 1].start(priority)
        copies[i].wait()
        chunk = stage_ref[i % 2]
        if scale_below is not None and i * size < scale_below:
            col = i * size + lax.broadcasted_iota(jnp.int32, (1, size), 1)
            chunk = chunk * jnp.where(col < scale_below, 0.5, 1.0)
        w_ref[where(i)] = chunk.astype(BF16)
        yield


def _stage_ffn_weights(wgu_hbm, wd_hbm, wgu_ref, wd_ref, stage_gu, stage_d, sem_gu, sem_d):
    @pl.when(pl.program_id(0) == 0)
    def _():
        streams = [
            _stage_weight(wgu_hbm, wgu_ref, stage_gu, sem_gu, axis=1, priority=0,
                          scale_below=wd_ref.shape[0]),
            _stage_weight(wd_hbm, wd_ref, stage_d, sem_d, axis=0, priority=1),
        ]
        while streams:
            for stream in list(streams):
                if next(stream, StopIteration) is StopIteration:
                    streams.remove(stream)


def _ffn_kernel(x_ref, gain_ref, wgu_hbm, wd_hbm, o_ref, h_ref, acc_ref, wgu_ref, wd_ref,
                stage_gu, stage_d, sem_gu, sem_d):
    _stage_ffn_weights(wgu_hbm, wd_hbm, wgu_ref, wd_ref, stage_gu, stage_d, sem_gu, sem_d)
    o_ref[...] = _ffn_half_step(x_ref, gain_ref, wgu_ref, wd_ref, h_ref, acc_ref)


def _ffn_ple_kernel(x_ref, gain_ref, wgu_hbm, wd_hbm, p_ref, pgain_ref, wgate_ref, bgate_ref,
                    wproj_ref, post_ref, o_ref, h_ref, acc_ref, wgu_ref, wd_ref,
                    stage_gu, stage_d, sem_gu, sem_d):
    _stage_ffn_weights(wgu_hbm, wd_hbm, wgu_ref, wd_ref, stage_gu, stage_d, sem_gu, sem_d)
    e = jnp.dot(p_ref[...].astype(BF16), wproj_ref[...], preferred_element_type=F32)
    x1 = _ffn_half_step(x_ref, gain_ref, wgu_ref, wd_ref, h_ref, acc_ref)
    half_e = 0.5 * _rms(e, post_ref[...])
    base = x1 + half_e
    hp = _rms(x1, pgain_ref[...]).astype(BF16)
    part = 2 * V7X_MXU_DIM
    for c0 in range(0, x1.shape[1], part):
        cols = slice(c0, c0 + part)
        half_z = jnp.dot(hp, wgate_ref[:, cols], preferred_element_type=F32) + bgate_ref[:, cols]
        o_ref[:, cols] = base[:, cols] + half_e[:, cols] * jnp.tanh(half_z)


def _ffn(x2, gain, w_gu, w_down, ple=None):
    n_tok, d = x2.shape
    d_ff = w_down.shape[0]
    assert d_ff % FFN_CHUNK == 0 and n_tok % FFN_ROWS == 0 and w_gu.shape == (d, 2 * d_ff)
    row_spec = pl.BlockSpec((FFN_ROWS, d), lambda s: (s, 0))
    in_hbm = pl.BlockSpec(memory_space=pl.ANY)
    in_specs = [row_spec, _const_spec((1, d)), in_hbm, in_hbm]
    args = [x2, gain.reshape(1, d), w_gu, w_down]
    gu_stage_cols = 2 * FFN_CHUNK
    assert (2 * d_ff) % gu_stage_cols == 0
    if ple is not None:
        p2, pgain, w_gate, b_gate, w_proj, post_gain = ple
        in_specs += [pl.BlockSpec((FFN_ROWS, p2.shape[1]), lambda s: (s, 0)), _const_spec((1, d)),
                     _const_spec(w_gate.shape), _const_spec((1, d)), _const_spec(w_proj.shape),
                     _const_spec((1, d))]
        args += [p2, pgain.reshape(1, d), (0.5 * w_gate).astype(BF16), (0.5 * b_gate).reshape(1, d),
                 w_proj.astype(BF16), post_gain.reshape(1, d)]
    return pl.pallas_call(
        _ffn_kernel if ple is None else _ffn_ple_kernel,
        name="ffn" if ple is None else "ffn_ple",
        grid=(n_tok // FFN_ROWS,),
        in_specs=in_specs,
        out_specs=row_spec,
        out_shape=jax.ShapeDtypeStruct((n_tok, d), F32),
        scratch_shapes=[pltpu.VMEM((FFN_ROWS, d), BF16), pltpu.VMEM((FFN_ROWS, d), F32),
                        pltpu.VMEM((d, 2 * d_ff), BF16), pltpu.VMEM((d_ff, d), BF16),
                        pltpu.VMEM((2, d, gu_stage_cols), F32), pltpu.VMEM((2, FFN_CHUNK, d), F32),
                        pltpu.SemaphoreType.DMA((2,)), pltpu.SemaphoreType.DMA((2,))],
        compiler_params=pltpu.CompilerParams(
            dimension_semantics=("arbitrary",), vmem_limit_bytes=FFN_VMEM_LIMIT),
    )(*args)


def _head_rms_cols(t, low, gain):
    heads_per_col = 2
    cols = []
    for c in range(t.shape[1] // V7X_LANES):
        sl = slice(c * V7X_LANES, (c + 1) * V7X_LANES)
        tc = t[:, sl]
        sq = tc * tc
        ss_low = jnp.sum(jnp.where(low, sq, 0.0), axis=-1, keepdims=True)
        ss_high = jnp.sum(jnp.where(low, 0.0, sq), axis=-1, keepdims=True)
        ms = jnp.where(low, ss_low, ss_high) * (heads_per_col / V7X_LANES)
        cols.append(tc * lax.rsqrt(ms + EPS) * gain[:, sl])
    return cols


def _dup_half(t, lane, half):
    r = pltpu.roll(t, half, axis=1)
    low = lane < half
    return jnp.where(low, t, r), jnp.where(low, r, t)


def _build_score_tables(bucket_ref, relb_ref, sinks_ref, tbl_ref, fill_ref):
    n_buckets, n_heads = relb_ref.shape
    step = F32_SUBLANES

    def row_group(r, carry):
        r0 = pl.multiple_of(r * step, step)
        bidx = bucket_ref[pl.ds(r0, step), :]
        slot0 = lax.broadcasted_iota(jnp.int32, bidx.shape, 1) == 0
        accs = [jnp.zeros(bidx.shape, F32) for _ in range(n_heads)]
        for b in range(n_buckets):
            hit = bidx == b
            accs = [jnp.where(hit, relb_ref[b, h], accs[h]) for h in range(n_heads)]
        for h in range(n_heads):
            tbl_ref[h, pl.ds(r0, step), :] = accs[h]
            fill_ref[h, pl.ds(r0, step), :] = jnp.where(slot0, sinks_ref[h], NEG_INF)
        return carry

    lax.fori_loop(0, WINDOW // step, row_group, 0)


def _mixer_kernel(bucket_ref, relb_ref, sinks_ref, x_ref, gain_ref, win_ref, qg_ref, kg_ref,
                  poolw_ref, pscale_ref, wout_ref, o_ref,
                  tbl_ref, fill_ref, qs_ref, kctx_ref, vctx_ref, uext_ref, cat_ref):
    blk = WINDOW
    n_seq, rows, _ = x_ref.shape
    n_blk = rows // blk
    n_heads = tbl_ref.shape[0]
    n_kv = kctx_ref.shape[1]
    group = n_heads // n_kv
    head_dim = V7X_LANES // 2
    attn_w = n_heads * head_dim
    kv_w = n_kv * head_dim
    pool_w = cat_ref.shape[-1] - attn_w
    pool_g = pool_w // len(POOL_WINDOWS)
    j = pl.program_id(1)

    @pl.when((pl.program_id(0) == 0) & (j == 0))
    def _first_step():
        _build_score_tables(bucket_ref, relb_ref, sinks_ref, tbl_ref, fill_ref)

    @pl.when(j == 0)
    def _reset_carry():
        kctx_ref[:, :, 0:blk, :] = jnp.zeros((n_seq, n_kv, blk, V7X_LANES), BF16)
        vctx_ref[:, :, 0:blk, :] = jnp.zeros((n_seq, n_kv, blk, 2 * V7X_LANES), BF16)
        uext_ref[:, 0:MAX_POOL, :] = jnp.zeros((n_seq, MAX_POOL, pool_w), F32)

    lane = lax.broadcasted_iota(jnp.int32, (rows, V7X_LANES), 1)
    low = lane < head_dim
    ones = jnp.ones((rows, V7X_LANES), BF16)
    qi = lax.broadcasted_iota(jnp.int32, (blk, 2 * blk), 0)
    kj = lax.broadcasted_iota(jnp.int32, (blk, 2 * blk), 1)
    dist = qi + blk - kj
    band = (dist >= 0) & (dist < WINDOW)
    low_blk = lax.broadcasted_iota(jnp.int32, (blk, V7X_LANES), 1) < head_dim
    t_pos = j * rows + lax.broadcasted_iota(jnp.int32, (rows, 1), 0)
    top_row = lax.broadcasted_iota(jnp.int32, (BF16_SUBLANES, 2 * V7X_LANES), 0) == 0
    sink_row = (lax.broadcasted_iota(jnp.int32, (BF16_SUBLANES, 2 * V7X_LANES), 1)
                >= V7X_LANES).astype(F32)

    projs = []
    for s in range(n_seq):
        h = _rms(x_ref[s], gain_ref[...]).astype(BF16)
        projs.append(jnp.dot(h, win_ref[...], preferred_element_type=F32))

    for s, proj in enumerate(projs):
        q = proj[:, 0:attn_w]
        k = proj[:, attn_w:attn_w + kv_w]
        v = proj[:, attn_w + kv_w:attn_w + 2 * kv_w]
        uext_ref[s, MAX_POOL:MAX_POOL + rows, :] = proj[:, attn_w + 2 * kv_w:]
        for c, qc in enumerate(_head_rms_cols(q, low, qg_ref[...])):
            q_lo = jnp.where(low, qc, 0.0).astype(BF16)
            q_hi = jnp.where(low, 0.0, qc).astype(BF16)
            kvh, slot = divmod(2 * c, group)
            for b in range(n_blk):
                qs_ref[s, kvh, b, slot * blk:(slot + 1) * blk, :] = q_lo[b * blk:(b + 1) * blk]
                qs_ref[s, kvh, b, (slot + 1) * blk:(slot + 2) * blk, :] = q_hi[b * blk:(b + 1) * blk]
        (kn,) = _head_rms_cols(k, low, kg_ref[...])
        kv_pairs = zip(_dup_half(kn, lane, head_dim), _dup_half(v, lane, head_dim))
        for kvh, (kd, vd) in enumerate(kv_pairs):
            kctx_ref[s, kvh, blk:blk + rows, :] = kd.astype(BF16)
            vctx_ref[s, kvh, blk:blk + rows, 0:V7X_LANES] = vd.astype(BF16)
            vctx_ref[s, kvh, blk:blk + rows, V7X_LANES:] = ones

    def scores(s, b, kvh):
        kc = kctx_ref[s, kvh, b * blk:(b + 2) * blk, :]
        return lax.dot_general(qs_ref[s, kvh, b], kc, (((1,), (1,)), ((), ())),
                               preferred_element_type=F32)

    def attend(sc, s, b, kvh):
        r0 = b * blk
        valid = band if b > 0 else band & ((kj >= blk) | (j > 0))
        es = []
        for i in range(group):
            head = kvh * group + i
            si = jnp.where(valid, sc[i * blk:(i + 1) * blk] + tbl_ref[head], fill_ref[head])
            m = jnp.max(si, axis=-1, keepdims=True)
            es.append(jnp.exp2(si - m).astype(BF16))
        top = vctx_ref[s, kvh, r0:r0 + BF16_SUBLANES, :].astype(F32)
        top = jnp.where(top_row, sink_row, top).astype(BF16)
        vc = jnp.concatenate([top, vctx_ref[s, kvh, r0 + BF16_SUBLANES:r0 + 2 * blk, :]], axis=0)
        pv = jnp.dot(jnp.concatenate(es, axis=0), vc, preferred_element_type=F32)
        outs = [pv[i * blk:(i + 1) * blk, 0:V7X_LANES] / pv[i * blk:(i + 1) * blk, V7X_LANES:]
                for i in range(group)]
        for pair in range(group // 2):
            col = (kvh * group + 2 * pair) * head_dim
            both = jnp.where(low_blk, outs[2 * pair], outs[2 * pair + 1])
            cat_ref[s, r0:r0 + blk, col:col + V7X_LANES] = both.astype(BF16)

    items = [(s, b, kvh) for b in range(n_blk) for kvh in range(n_kv) for s in range(n_seq)]
    sc_next = scores(*items[0])
    for n, item in enumerate(items):
        sc = sc_next
        if n + 1 < len(items):
            sc_next = scores(*items[n + 1])
        attend(sc, *item)

    for g, w in enumerate(POOL_WINDOWS):
        cols = slice(g * pool_g, (g + 1) * pool_g)
        cnt = jnp.minimum(t_pos + 1, w).astype(F32)
        for s in range(n_seq):
            total = uext_ref[s, :, cols]
            shift = 1
            while shift < w:
                total = total + pltpu.roll(total, shift, axis=0)
                shift *= 2
            d = total[MAX_POOL:] * (1.0 / cnt) - uext_ref[s, MAX_POOL:, cols]
            y = jnp.dot(d.astype(BF16), poolw_ref[g], preferred_element_type=F32)
            cat_ref[s, :, attn_w + g * pool_g:attn_w + (g + 1) * pool_g] = (
                y * pscale_ref[:, cols]).astype(BF16)

    for s in range(n_seq):
        o_ref[s] = x_ref[s] + jnp.dot(cat_ref[s], wout_ref[...], preferred_element_type=F32)
    kctx_ref[:, :, 0:blk, :] = kctx_ref[:, :, rows:rows + blk, :]
    vctx_ref[:, :, 0:blk, :] = vctx_ref[:, :, rows:rows + blk, :]
    uext_ref[:, 0:MAX_POOL, :] = uext_ref[:, rows:rows + MAX_POOL, :]


def _t5_bucket_table(n_buckets):
    qi = jnp.arange(WINDOW)[:, None]
    kj = jnp.arange(2 * WINDOW)[None, :]
    n = jnp.maximum(qi + WINDOW - kj, 0)
    max_exact = n_buckets // 2
    nf = jnp.maximum(n, 1).astype(F32)
    large = max_exact + (jnp.log(nf / max_exact) / math.log(MAX_DISTANCE / max_exact)
                         * (n_buckets - max_exact)).astype(jnp.int32)
    large = jnp.minimum(large, n_buckets - 1)
    return jnp.where(n < max_exact, n, large).astype(jnp.int32)


def _mixer(x2, seq, gain, w_in, q_norm, k_norm, rel_bias, sinks, pool_w, pool_scale, w_out):
    n_tok, d = x2.shape
    head_dim = q_norm.shape[0]
    n_buckets, n_heads = rel_bias.shape
    n_groups, pool_g, _ = pool_w.shape
    attn_w = n_heads * head_dim
    pool_width = n_groups * pool_g
    kv_w = (w_in.shape[1] - attn_w - pool_width) // 2
    n_kv = kv_w // head_dim
    group = n_heads // n_kv
    rows = TILE_ROWS
    n_seq = MIX_SEQS
    batch = n_tok // seq
    assert n_groups == len(POOL_WINDOWS) and pool_g == V7X_LANES
    assert 2 * head_dim == V7X_LANES and kv_w == V7X_LANES and group % 2 == 0
    assert seq % rows == 0 and rows % WINDOW == 0 and w_out.shape[0] == attn_w + pool_width
    assert batch % n_seq == 0

    q_gain = jnp.tile(q_norm, n_heads) * (LOG2E / math.sqrt(head_dim))

    row_spec = pl.BlockSpec((None, n_seq, rows, d), lambda b, j: (b, 0, j, 0))
    smem = pl.BlockSpec(memory_space=pltpu.SMEM)
    out = pl.pallas_call(
        _mixer_kernel,
        name="mixer",
        grid=(batch // n_seq, seq // rows),
        in_specs=[
            _const_spec((WINDOW, 2 * WINDOW)), smem, smem,
            row_spec, _const_spec((1, d)), _const_spec(w_in.shape),
            _const_spec((1, attn_w)), _const_spec((1, kv_w)),
            _const_spec(pool_w.shape), _const_spec((1, pool_width)), _const_spec(w_out.shape),
        ],
        out_specs=row_spec,
        out_shape=jax.ShapeDtypeStruct((batch // n_seq, n_seq, seq, d), F32),
        scratch_shapes=[
            pltpu.VMEM((n_heads, WINDOW, 2 * WINDOW), F32),
            pltpu.VMEM((n_heads, WINDOW, 2 * WINDOW), F32),
            pltpu.VMEM((n_seq, n_kv, rows // WINDOW, group * WINDOW, V7X_LANES), BF16),
            pltpu.VMEM((n_seq, n_kv, WINDOW + rows, V7X_LANES), BF16),
            pltpu.VMEM((n_seq, n_kv, WINDOW + rows, 2 * V7X_LANES), BF16),
            pltpu.VMEM((n_seq, MAX_POOL + rows, pool_width), F32),
            pltpu.VMEM((n_seq, rows, attn_w + pool_width), BF16),
        ],
        compiler_params=pltpu.CompilerParams(
            dimension_semantics=("arbitrary", "arbitrary"), vmem_limit_bytes=MIX_VMEM_LIMIT),
    )(_t5_bucket_table(n_buckets), rel_bias * LOG2E, sinks * LOG2E,
      x2.reshape(batch // n_seq, n_seq, seq, d), gain.reshape(1, d), w_in.astype(BF16),
      q_gain.reshape(1, attn_w), jnp.tile(k_norm, n_kv).reshape(1, kv_w),
      pool_w.astype(BF16), pool_scale.reshape(1, pool_width), w_out.astype(BF16))
    return out.reshape(n_tok, d)


def kernel(x, p, ffn1_norm, ffn1_w_gu, ffn1_w_down, mix_norm, w_in, q_norm, k_norm, rel_bias,
           sinks, pool_w, pool_scale, w_out, ffn2_norm, ffn2_w_gu, ffn2_w_down, ple_norm,
           ple_w_gate, ple_b_gate, ple_w_proj, ple_post_norm):
    batch, seq, d = x.shape
    x2 = x.reshape(batch * seq, d)
    for i in range(p.shape[0]):
        x2 = _ffn(x2, ffn1_norm[i], ffn1_w_gu[i], ffn1_w_down[i])
        x2 = _mixer(x2, seq, mix_norm[i], w_in[i], q_norm[i], k_norm[i], rel_bias, sinks[i],
                    pool_w[i], pool_scale[i], w_out[i])
        x2 = _ffn(x2, ffn2_norm[i], ffn2_w_gu[i], ffn2_w_down[i],
                  ple=(p[i].reshape(batch * seq, -1), ple_norm[i], ple_w_gate[i], ple_b_gate[i],
                       ple_w_proj[i], ple_post_norm[i]))
    return x2.reshape(batch, seq, d)
```
